```python
import math
import jax, jax.numpy as jnp
from jax import lax
import numpy as np

D_MODEL = 1024
BATCH = 8
SEQ = 2048
DEPTH = 4
DEC_BATCH = 128
DEC_SEQ = 1
PAST_LEN = 16384
PAGE_SIZE = 128

MIX_WIDTH = D_MODEL
GROUP_WIDTH = MIX_WIDTH // 2
LRU_HEADS = 8
LRU_HEAD_DIM = GROUP_WIDTH // LRU_HEADS
LRU_C = 8.0
CONV_W = 4
POOL_WINDOWS = (2, 4, 8, 16)
POOL_GROUPS = len(POOL_WINDOWS)
POOL_GROUP = GROUP_WIDTH // POOL_GROUPS
POOL_MAX_WIN = max(POOL_WINDOWS)
D_FF = 2816
LN_EPS = 1e-5
DEEPNORM_ALPHA = (2.0 * DEPTH) ** 0.25
DEEPNORM_BETA = (8.0 * DEPTH) ** -0.25

kernel_name = "hybrid_rglru_pool_macaron_deepnorm_step"


def _layernorm(x, g, b):
    xf = x.astype(jnp.float32)
    mu = jnp.mean(xf, axis=-1, keepdims=True)
    var = jnp.mean(jnp.square(xf - mu), axis=-1, keepdims=True)
    y = (xf - mu) * lax.rsqrt(var + LN_EPS) * g.astype(jnp.float32) + b.astype(jnp.float32)
    return y.astype(x.dtype)


def _swiglu(x, w_gate, w_up, w_down):
    hid = jax.nn.silu(jnp.einsum("btd,df->btf", x, w_gate)) * jnp.einsum("btd,df->btf", x, w_up)
    return jnp.einsum("btf,fd->btd", hid, w_down)


def _causal_conv(u, prefix, w, b):
    T = u.shape[1]
    up = jnp.concatenate([prefix.astype(u.dtype), u], axis=1)
    out = b + up[:, 0:T] * w[0]
    for k in range(1, CONV_W):
        out = out + up[:, k:k + T] * w[k]
    return out, up[:, up.shape[1] - (CONV_W - 1):]


def _rglru(xc, h0, gate_a_w, gate_a_b, gate_x_w, gate_x_b, lam):
    B, T, C = xc.shape
    xh = xc.reshape(B, T, LRU_HEADS, LRU_HEAD_DIM)
    r = jax.nn.sigmoid(jnp.einsum("bthi,hij->bthj", xh, gate_a_w).reshape(B, T, C) + gate_a_b)
    i = jax.nn.sigmoid(jnp.einsum("bthi,hij->bthj", xh, gate_x_w).reshape(B, T, C) + gate_x_b)
    log_a = -LRU_C * r.astype(jnp.float32) * jax.nn.softplus(-lam.astype(jnp.float32))
    a = jnp.exp(log_a)
    mult = jnp.sqrt(-jnp.expm1(2.0 * log_a))
    u = mult * (i * xc).astype(jnp.float32)

    def step(h, au):
        a_t, u_t = au
        h = a_t * h + u_t
        return h, h

    h_T, hs = lax.scan(step, h0.astype(jnp.float32),
                       (jnp.swapaxes(a, 0, 1), jnp.swapaxes(u, 0, 1)))
    return jnp.swapaxes(hs, 0, 1).astype(xc.dtype), h_T.astype(xc.dtype)


def _pool_mix(u, prefix, start_pos, pool_w, pool_scale):
    B, T, C = u.shape
    P = POOL_MAX_WIN - 1
    up = jnp.concatenate([prefix.astype(u.dtype), u], axis=1)
    cs = jnp.cumsum(up.astype(jnp.float32), axis=1)
    cs = jnp.concatenate([jnp.zeros((B, 1, C), jnp.float32), cs], axis=1)
    end = cs[:, P + 1:]
    pos = start_pos + jnp.arange(T, dtype=jnp.int32)
    means = []
    for g, w in enumerate(POOL_WINDOWS):
        lo, hi = g * POOL_GROUP, (g + 1) * POOL_GROUP
        begin = cs[:, P + 1 - w:P + 1 - w + T, lo:hi]
        cnt = jnp.minimum(w, pos + 1).astype(jnp.float32)[None, :, None]
        means.append((end[:, :, lo:hi] - begin) / cnt)
    pooled = (jnp.concatenate(means, axis=-1) - u.astype(jnp.float32)).astype(u.dtype)
    mixed = jnp.einsum("btgi,gij->btgj", pooled.reshape(B, T, POOL_GROUPS, POOL_GROUP), pool_w)
    return mixed.reshape(B, T, C) * pool_scale, up[:, up.shape[1] - P:]


def _layer(x, h0, conv_buf, pool_buf, start_pos,
           ln_g, ln_b, w1_gate, w1_up, w1_down, w_in, conv_w, conv_b,
           gate_a_w, gate_a_b, gate_x_w, gate_x_b, lru_lambda, pool_w, pool_scale,
           w_out, w2_gate, w2_up, w2_down):
    x = _layernorm(DEEPNORM_ALPHA * x + 0.5 * _swiglu(x, w1_gate, w1_up, w1_down), ln_g[0], ln_b[0])
    proj = jnp.einsum("btd,de->bte", x, w_in)
    u_lru = proj[..., :GROUP_WIDTH]
    u_gate = proj[..., GROUP_WIDTH:2 * GROUP_WIDTH]
    u_pool = proj[..., 2 * GROUP_WIDTH:]
    xc, conv_new = _causal_conv(u_lru, conv_buf, conv_w, conv_b)
    h, h_T = _rglru(xc, h0, gate_a_w, gate_a_b, gate_x_w, gate_x_b, lru_lambda)
    y_lru = h * jax.nn.gelu(u_gate)
    y_pool, pool_new = _pool_mix(u_pool, pool_buf, start_pos, pool_w, pool_scale)
    mix = jnp.einsum("bte,ed->btd", jnp.concatenate([y_lru, y_pool], axis=-1), w_out)
    x = _layernorm(DEEPNORM_ALPHA * x + mix, ln_g[1], ln_b[1])
    x = _layernorm(DEEPNORM_ALPHA * x + 0.5 * _swiglu(x, w2_gate, w2_up, w2_down), ln_g[2], ln_b[2])
    return x, h_T, conv_new, pool_new


def setup_inputs(seed: int = 0) -> dict:
    key = jax.random.key(seed)
    ks = jax.random.split(key, 32)
    f32 = jnp.float32
    nrm = lambda k, shape, s: jax.random.normal(k, shape, f32) * s
    x_prompt = nrm(ks[0], (BATCH, SEQ, D_MODEL), 1.0)
    x_sample = nrm(ks[1], (DEC_BATCH, DEC_SEQ, D_MODEL), 1.0)
    state_lru_h = nrm(ks[2], (DEPTH, DEC_BATCH, GROUP_WIDTH), 0.5)
    state_conv = nrm(ks[3], (DEPTH, DEC_BATCH, CONV_W - 1, GROUP_WIDTH), 1.0)
    state_pool = nrm(ks[4], (DEPTH, DEC_BATCH, POOL_MAX_WIN - 1, GROUP_WIDTH), 1.0)
    ln_g = 1.0 + nrm(ks[5], (DEPTH, 3, D_MODEL), 0.05)
    ln_b = nrm(ks[6], (DEPTH, 3, D_MODEL), 0.02)
    w1_gate = nrm(ks[7], (DEPTH, D_MODEL, D_FF), D_MODEL ** -0.5)
    w1_up = nrm(ks[8], (DEPTH, D_MODEL, D_FF), D_MODEL ** -0.5)
    w1_down = nrm(ks[9], (DEPTH, D_FF, D_MODEL), DEEPNORM_BETA * D_FF ** -0.5)
    w_in = nrm(ks[10], (DEPTH, D_MODEL, 3 * GROUP_WIDTH), D_MODEL ** -0.5)
    conv_w = nrm(ks[11], (DEPTH, CONV_W, GROUP_WIDTH), CONV_W ** -0.5)
    conv_b = nrm(ks[12], (DEPTH, GROUP_WIDTH), 0.02)
    gate_a_w = nrm(ks[13], (DEPTH, LRU_HEADS, LRU_HEAD_DIM, LRU_HEAD_DIM), LRU_HEAD_DIM ** -0.5)
    gate_a_b = nrm(ks[14], (DEPTH, GROUP_WIDTH), 0.02)
    gate_x_w = nrm(ks[15], (DEPTH, LRU_HEADS, LRU_HEAD_DIM, LRU_HEAD_DIM), LRU_HEAD_DIM ** -0.5)
    gate_x_b = nrm(ks[16], (DEPTH, GROUP_WIDTH), 0.02)
    a0 = jax.random.uniform(ks[17], (DEPTH, GROUP_WIDTH), f32, 0.9, 0.999)
    s = a0 ** (1.0 / LRU_C)
    lru_lambda = jnp.log(s) - jnp.log1p(-s)
    pool_w = nrm(ks[18], (DEPTH, POOL_GROUPS, POOL_GROUP, POOL_GROUP), POOL_GROUP ** -0.5)
    pool_scale = 1.0 + nrm(ks[19], (DEPTH, GROUP_WIDTH), 0.1)
    w_out = nrm(ks[20], (DEPTH, MIX_WIDTH, D_MODEL), DEEPNORM_BETA * MIX_WIDTH ** -0.5)
    w2_gate = nrm(ks[21], (DEPTH, D_MODEL, D_FF), D_MODEL ** -0.5)
    w2_up = nrm(ks[22], (DEPTH, D_MODEL, D_FF), D_MODEL ** -0.5)
    w2_down = nrm(ks[23], (DEPTH, D_FF, D_MODEL), DEEPNORM_BETA * D_FF ** -0.5)
    return {"x_prompt": x_prompt, "x_sample": x_sample,
            "state_lru_h": state_lru_h, "state_conv": state_conv, "state_pool": state_pool,
            "ln_g": ln_g, "ln_b": ln_b,
            "w1_gate": w1_gate, "w1_up": w1_up, "w1_down": w1_down,
            "w_in": w_in, "conv_w": conv_w, "conv_b": conv_b,
            "gate_a_w": gate_a_w, "gate_a_b": gate_a_b,
            "gate_x_w": gate_x_w, "gate_x_b": gate_x_b, "lru_lambda": lru_lambda,
            "pool_w": pool_w, "pool_scale": pool_scale, "w_out": w_out,
            "w2_gate": w2_gate, "w2_up": w2_up, "w2_down": w2_down}


def reference(x_prompt, x_sample, state_lru_h, state_conv, state_pool,
              ln_g, ln_b, w1_gate, w1_up, w1_down, w_in, conv_w, conv_b,
              gate_a_w, gate_a_b, gate_x_w, gate_x_b, lru_lambda, pool_w, pool_scale,
              w_out, w2_gate, w2_up, w2_down):
    dt = x_prompt.dtype
    xp, xs = x_prompt, x_sample
    p_h, p_conv, p_pool, s_h, s_conv, s_pool = [], [], [], [], [], []
    for l in range(DEPTH):
        params = (ln_g[l], ln_b[l], w1_gate[l], w1_up[l], w1_down[l], w_in[l], conv_w[l], conv_b[l],
                  gate_a_w[l], gate_a_b[l], gate_x_w[l], gate_x_b[l], lru_lambda[l],
                  pool_w[l], pool_scale[l], w_out[l], w2_gate[l], w2_up[l], w2_down[l])
        xp, h_T, cbuf, pbuf = _layer(
            xp, jnp.zeros((BATCH, GROUP_WIDTH), dt),
            jnp.zeros((BATCH, CONV_W - 1, GROUP_WIDTH), dt),
            jnp.zeros((BATCH, POOL_MAX_WIN - 1, GROUP_WIDTH), dt),
            0, *params)
        p_h.append(h_T); p_conv.append(cbuf); p_pool.append(pbuf)
        xs, h_T, cbuf, pbuf = _layer(
            xs, state_lru_h[l], state_conv[l], state_pool[l], PAST_LEN, *params)
        s_h.append(h_T); s_conv.append(cbuf); s_pool.append(pbuf)
    return (xp, xs,
            jnp.stack(p_h), jnp.stack(p_conv), jnp.stack(p_pool),
            jnp.stack(s_h), jnp.stack(s_conv), jnp.stack(s_pool))
```

```python
import functools
import math

import jax
import jax.numpy as jnp
from jax import lax
from jax.experimental import pallas as pl
from jax.experimental.pallas import tpu as pltpu

D_MODEL = 1024
DEPTH = 4
GROUP_WIDTH = 512
LRU_HEADS = 8
LRU_C = 8.0
CONV_W = 4
POOL_WINDOWS = (2, 4, 8, 16)
POOL_MAX_WIN = 16
D_FF = 2816
LN_EPS = 1e-5
DEEPNORM_ALPHA = (2.0 * DEPTH) ** 0.25
PAST_LEN = 16384

V7X_LANES = 128
V7X_SUBLANES = 8
V7X_MXU_DIM = 256
V7X_VMEM_LIMIT_BYTES = 56 * 1024 * 1024

HIST = POOL_MAX_WIN
FFN_TOKEN_TILE = 512
MIXER_TIME_TILE = 64
FFN_CHUNKS = (768, 768, 768, 512)
assert sum(FFN_CHUNKS) == D_FF and all(c % V7X_MXU_DIM == 0 for c in FFN_CHUNKS)

_BF16 = jnp.bfloat16
_F32 = jnp.float32


def _dot(a, b):
    return jnp.dot(a, b, preferred_element_type=_F32)


def _layernorm(y, g, b):
    mu = jnp.mean(y, axis=-1, keepdims=True)
    d = y - mu
    var = jnp.mean(d * d, axis=-1, keepdims=True)
    return d * lax.rsqrt(var + LN_EPS) * g + b


def _resident(shape):
    nd = len(shape)
    return pl.BlockSpec(shape, lambda *_: (0,) * nd, pipeline_mode=pl.Buffered(1))


def _ffn_kernel(x_ref, wg_ref, wu_ref, wd_ref, g_ref, b_ref, o_ref):
    x = x_ref[...]
    xb = x.astype(_BF16)
    acc = jnp.zeros(x.shape, _F32)
    lo = 0
    for width in FFN_CHUNKS:
        gate = _dot(xb, wg_ref[:, lo:lo + width])
        up = _dot(xb, wu_ref[:, lo:lo + width])
        hid = (jax.nn.silu(gate) * up).astype(_BF16)
        acc = acc + _dot(hid, wd_ref[lo:lo + width, :])
        lo += width
    y = DEEPNORM_ALPHA * x + 0.5 * acc
    o_ref[...] = _layernorm(y, g_ref[...], b_ref[...])


def _ffn(x, wg, wu, wd, g, b):
    m = x.shape[0]
    tm = min(FFN_TOKEN_TILE, m)
    assert m % tm == 0
    return pl.pallas_call(
        _ffn_kernel,
        grid=(m // tm,),
        in_specs=[
            pl.BlockSpec((tm, D_MODEL), lambda i: (i, 0)),
            _resident(wg.shape), _resident(wu.shape), _resident(wd.shape),
            _resident(g.shape), _resident(b.shape),
        ],
        out_specs=pl.BlockSpec((tm, D_MODEL), lambda i: (i, 0)),
        out_shape=jax.ShapeDtypeStruct((m, D_MODEL), _F32),
        compiler_params=pltpu.CompilerParams(
            dimension_semantics=("arbitrary",),
            vmem_limit_bytes=V7X_VMEM_LIMIT_BYTES),
        name="ffn",
    )(x, wg, wu, wd, g, b)


def _mixer_kernel(x_ref, h0_ref, cst_ref, pst_ref, win_ref, cw_ref, cb_ref,
                  gw_ref, gab_ref, gxb_ref, lam_ref, pw_ref, ps_ref, wout_ref,
                  g_ref, b_ref,
                  o_ref, ht_ref, cnew_ref, pnew_ref,
                  ul_ext, up_ext, h_scr, a_scr, uu_scr, hs_scr,
                  *, nb, tt, start_pos):
    c = pl.program_id(0)
    m = tt * nb
    hr = HIST * nb
    gw = GROUP_WIDTH

    @pl.when(c == 0)
    def _():
        h_scr[...] = h0_ref[...]
        ul_ext[0:hr, :] = cst_ref[...]
        up_ext[0:hr, :] = pst_ref[...]

    x = x_ref[...]
    proj = _dot(x.astype(_BF16), win_ref[...])
    u_lru = proj[:, 0:gw]
    u_gate = proj[:, gw:2 * gw]
    u_pool = proj[:, 2 * gw:3 * gw]
    ul_ext[hr:hr + m, :] = u_lru
    up_ext[hr:hr + m, :] = u_pool

    xc = cb_ref[...] + cw_ref[CONV_W - 1:CONV_W, :] * u_lru
    for k in range(CONV_W - 1):
        off = hr - (CONV_W - 1 - k) * nb
        xc = xc + cw_ref[k:k + 1, :] * ul_ext[off:off + m, :]

    xcb = xc.astype(_BF16)
    half = gw // 2
    ga, gx = [], []
    for p in range(2):
        gts = _dot(xcb[:, p * half:(p + 1) * half], gw_ref[p])
        ga.append(gts[:, 0:half])
        gx.append(gts[:, half:2 * half])
    r = jax.nn.sigmoid(jnp.concatenate(ga, axis=-1) + gab_ref[...])
    i = jax.nn.sigmoid(jnp.concatenate(gx, axis=-1) + gxb_ref[...])
    nl = -lam_ref[...]
    softplus_nl = jnp.maximum(nl, 0.0) + jnp.log1p(jnp.exp(-jnp.abs(nl)))
    log_a = -LRU_C * r * softplus_nl
    a = jnp.exp(log_a)
    mult = jnp.sqrt(1.0 - a * a)
    a_scr[...] = a
    uu_scr[...] = mult * (i * xc)

    def step(t, h):
        rows = pl.ds(pl.multiple_of(t * nb, V7X_SUBLANES), nb)
        h = a_scr[rows, :] * h + uu_scr[rows, :]
        hs_scr[rows, :] = h
        return h

    h_last = lax.fori_loop(0, tt, step, h_scr[...], unroll=min(tt, 8))
    h_scr[...] = h_last
    ht_ref[...] = h_last
    y_lru = hs_scr[...] * jax.nn.gelu(u_gate)

    t_loc = lax.broadcasted_iota(jnp.int32, (tt, nb, V7X_LANES), 0).reshape(m, V7X_LANES)
    pos = start_pos + c * tt + t_loc
    pooled = []
    for gi, w in enumerate(POOL_WINDOWS):
        lanes = slice(gi * V7X_LANES, (gi + 1) * V7X_LANES)
        n_rows = (w - 1) * nb + m
        s = up_ext[hr - (w - 1) * nb:hr + m, lanes]
        span = 1
        while span < w:
            n_rows -= span * nb
            s = s[span * nb:span * nb + n_rows, :] + s[0:n_rows, :]
            span *= 2
        cnt = jnp.minimum(w, pos + 1).astype(_F32)
        pooled.append(s / cnt - u_pool[:, lanes])
    pooled = jnp.concatenate(pooled, axis=-1).astype(_BF16)
    y_pool = jnp.concatenate(
        [_dot(pooled[:, p * half:(p + 1) * half], pw_ref[p]) for p in range(2)],
        axis=-1) * ps_ref[...]

    mix = _dot(jnp.concatenate([y_lru, y_pool], axis=-1).astype(_BF16), wout_ref[...])
    y = DEEPNORM_ALPHA * x + mix
    o_ref[...] = _layernorm(y, g_ref[...], b_ref[...])

    ul_tail = ul_ext[m:m + hr, :]
    up_tail = up_ext[m:m + hr, :]
    ul_ext[0:hr, :] = ul_tail
    up_ext[0:hr, :] = up_tail
    cnew_ref[...] = ul_tail
    pnew_ref[...] = up_tail


def _mixer(x, h0, cst, pst, win, cw, cb, gwt, gab, gxb, lam, pw, ps, wout, g, b,
           *, nb, start_pos):
    m_total = x.shape[0]
    t_total = m_total // nb
    tt = min(MIXER_TIME_TILE, t_total)
    assert t_total % tt == 0 and nb % V7X_SUBLANES == 0
    m = tt * nb
    hr = HIST * nb
    gw = GROUP_WIDTH
    kern = functools.partial(_mixer_kernel, nb=nb, tt=tt, start_pos=start_pos)
    params = (win, cw, cb, gwt, gab, gxb, lam, pw, ps, wout, g, b)
    return pl.pallas_call(
        kern,
        grid=(t_total // tt,),
        in_specs=[pl.BlockSpec((m, D_MODEL), lambda i: (i, 0)),
                  _resident(h0.shape), _resident(cst.shape), _resident(pst.shape)]
                 + [_resident(p.shape) for p in params],
        out_specs=[pl.BlockSpec((m, D_MODEL), lambda i: (i, 0)),
                   pl.BlockSpec((nb, gw), lambda i: (0, 0)),
                   pl.BlockSpec((hr, gw), lambda i: (0, 0)),
                   pl.BlockSpec((hr, gw), lambda i: (0, 0))],
        out_shape=[jax.ShapeDtypeStruct((m_total, D_MODEL), _F32),
                   jax.ShapeDtypeStruct((nb, gw), _F32),
                   jax.ShapeDtypeStruct((hr, gw), _F32),
                   jax.ShapeDtypeStruct((hr, gw), _F32)],
        scratch_shapes=[pltpu.VMEM((hr + m, gw), _F32),
                        pltpu.VMEM((hr + m, gw), _F32),
                        pltpu.VMEM((nb, gw), _F32),
                        pltpu.VMEM((m, gw), _F32),
                        pltpu.VMEM((m, gw), _F32),
                        pltpu.VMEM((m, gw), _F32)],
        compiler_params=pltpu.CompilerParams(
            dimension_semantics=("arbitrary",),
            vmem_limit_bytes=V7X_VMEM_LIMIT_BYTES),
        name="mixer",
    )(x, h0, cst, pst, *params)


def _block_diag_halves(w):
    depth, groups, n, _ = w.shape
    eye = jnp.eye(groups, dtype=w.dtype)
    full = (w[:, :, :, None, :] * eye[None, :, None, :, None]).reshape(
        depth, groups * n, groups * n)
    h = groups * n // 2
    return jnp.stack([full[:, :h, :h], full[:, h:, h:]], axis=1)


def _state_to_time_major(state):
    nb, k, c = state.shape
    tm = jnp.transpose(state, (1, 0, 2))
    tm = jnp.concatenate([jnp.zeros((HIST - k, nb, c), state.dtype), tm], axis=0)
    return tm.reshape(HIST * nb, c)


def _state_from_time_major(hist, nb, k):
    tm = hist.reshape(HIST, nb, hist.shape[-1])[HIST - k:]
    return jnp.transpose(tm, (1, 0, 2))


def kernel(x_prompt, x_sample, state_lru_h, state_conv, state_pool, ln_g, ln_b,
           w1_gate, w1_up, w1_down, w_in, conv_w, conv_b, gate_a_w, gate_a_b,
           gate_x_w, gate_x_b, lru_lambda, pool_w, pool_scale, w_out,
           w2_gate, w2_up, w2_down):
    batch, seq, d = x_prompt.shape
    dec_batch = x_sample.shape[0]
    gw = GROUP_WIDTH
    row = lambda v: v.reshape(1, -1)

    w1g, w1u, w1d = (w.astype(_BF16) for w in (w1_gate, w1_up, w1_down))
    w2g, w2u, w2d = (w.astype(_BF16) for w in (w2_gate, w2_up, w2_down))
    win = w_in.astype(_BF16)
    wout = w_out.astype(_BF16)
    gate_w = jnp.concatenate(
        [_block_diag_halves(gate_a_w), _block_diag_halves(gate_x_w)], axis=-1).astype(_BF16)
    pool_wb = _block_diag_halves(pool_w).astype(_BF16)

    xp = jnp.transpose(x_prompt, (1, 0, 2)).reshape(seq * batch, d)
    xs = x_sample.reshape(dec_batch, d)
    zeros_h = jnp.zeros((batch, gw), _F32)
    zeros_hist = jnp.zeros((HIST * batch, gw), _F32)

    p_h, p_conv, p_pool, s_h, s_conv, s_pool = [], [], [], [], [], []
    for l in range(DEPTH):
        ffn1 = (w1g[l], w1u[l], w1d[l], row(ln_g[l, 0]), row(ln_b[l, 0]))
        ffn2 = (w2g[l], w2u[l], w2d[l], row(ln_g[l, 2]), row(ln_b[l, 2]))
        mix = (win[l], conv_w[l], row(conv_b[l]), gate_w[l], row(gate_a_b[l]),
               row(gate_x_b[l]), row(lru_lambda[l]), pool_wb[l], row(pool_scale[l]),
               wout[l], row(ln_g[l, 1]), row(ln_b[l, 1]))

        xp = _ffn(xp, *ffn1)
        xp, h_t, cnew, pnew = _mixer(xp, zeros_h, zeros_hist, zeros_hist, *mix,
                                     nb=batch, start_pos=0)
        xp = _ffn(xp, *ffn2)
        p_h.append(h_t)
        p_conv.append(_state_from_time_major(cnew, batch, CONV_W - 1))
        p_pool.append(_state_from_time_major(pnew, batch, POOL_MAX_WIN - 1))

        xs = _ffn(xs, *ffn1)
        xs, h_t, cnew, pnew = _mixer(
            xs, state_lru_h[l], _state_to_time_major(state_conv[l]),
            _state_to_time_major(state_pool[l]), *mix, nb=dec_batch, start_pos=PAST_LEN)
        xs = _ffn(xs, *ffn2)
        s_h.append(h_t)
        s_conv.append(_state_from_time_major(cnew, dec_batch, CONV_W - 1))
        s_pool.append(_state_from_time_major(pnew, dec_batch, POOL_MAX_WIN - 1))

    y_prompt = jnp.transpose(xp.reshape(seq, batch, d), (1, 0, 2))
    y_sample = xs.reshape(dec_batch, 1, d)
    return (y_prompt, y_sample,
            jnp.stack(p_h), jnp.stack(p_conv), jnp.stack(p_pool),
            jnp.stack(s_h), jnp.stack(s_conv), jnp.stack(s_pool))
```

```python
import functools

import jax
import jax.numpy as jnp
from jax import lax
from jax.experimental import pallas as pl
from jax.experimental.pallas import tpu as pltpu

D_MODEL = 1024
DEPTH = 4
GROUP_WIDTH = 512
LRU_C = 8.0
CONV_W = 4
POOL_WINDOWS = (2, 4, 8, 16)
POOL_MAX_WIN = 16
D_FF = 2816
LN_EPS = 1e-5
DEEPNORM_ALPHA = (2.0 * DEPTH) ** 0.25
PAST_LEN = 16384

V7X_LANES = 128
V7X_SUBLANES = 8
V7X_MXU_DIM = 256
V7X_VMEM_LIMIT_BYTES = 56 * 1024 * 1024

HIST = POOL_MAX_WIN
FFN_TOKEN_TILE = 384
MIXER_TIME_TILE = 64
FFN_CHUNKS = (768, 768, 768, 512)
assert sum(FFN_CHUNKS) == D_FF and all(c % V7X_MXU_DIM == 0 for c in FFN_CHUNKS)

_BF16 = jnp.bfloat16
_F32 = jnp.float32


def _dot(a, b):
    return jnp.dot(a, b, preferred_element_type=_F32)


def _layernorm(y, g, b):
    mu = jnp.mean(y, axis=-1, keepdims=True)
    d = y - mu
    var = jnp.mean(d * d, axis=-1, keepdims=True)
    return d * lax.rsqrt(var + LN_EPS) * g + b


def _resident(shape):
    nd = len(shape)
    return pl.BlockSpec(shape, lambda *_: (0,) * nd, pipeline_mode=pl.Buffered(1))


def _layer_resident(shape, layer):
    nd = len(shape)
    return pl.BlockSpec((None,) + tuple(shape[1:]), lambda *_: (layer,) + (0,) * (nd - 1),
                        pipeline_mode=pl.Buffered(1))


def _ffn_kernel(x_ref, wg_ref, wu_ref, wd_ref, g_ref, b_ref, o_ref, *, ln_row):
    x = x_ref[...]
    xb = x.astype(_BF16)
    acc = jnp.zeros(x.shape, _F32)
    lo = 0
    for width in FFN_CHUNKS:
        gate = _dot(xb, wg_ref[:, lo:lo + width])
        up = _dot(xb, wu_ref[:, lo:lo + width])
        hid = (jax.nn.silu(gate) * up).astype(_BF16)
        acc = acc + _dot(hid, wd_ref[lo:lo + width, :])
        lo += width
    y = DEEPNORM_ALPHA * x + 0.5 * acc
    o_ref[...] = _layernorm(y, g_ref[ln_row:ln_row + 1, :], b_ref[ln_row:ln_row + 1, :])


def _ffn(x, wg, wu, wd, ln_g, ln_b, *, layer, ln_row):
    m = x.shape[0]
    tm = FFN_TOKEN_TILE
    assert m % tm == 0
    return pl.pallas_call(
        functools.partial(_ffn_kernel, ln_row=ln_row),
        grid=(m // tm,),
        in_specs=[
            pl.BlockSpec((tm, D_MODEL), lambda i: (i, 0)),
            _layer_resident(wg.shape, layer), _layer_resident(wu.shape, layer),
            _layer_resident(wd.shape, layer),
            _layer_resident(ln_g.shape, layer), _layer_resident(ln_b.shape, layer),
        ],
        out_specs=pl.BlockSpec((tm, D_MODEL), lambda i: (i, 0)),
        out_shape=jax.ShapeDtypeStruct((m, D_MODEL), _F32),
        compiler_params=pltpu.CompilerParams(
            dimension_semantics=("arbitrary",),
            vmem_limit_bytes=V7X_VMEM_LIMIT_BYTES),
        name="ffn",
    )(x, wg, wu, wd, ln_g, ln_b)


def _mixer_kernel(*refs, nb, tt, start_pos, layer, aliased):
    if aliased:
        refs = refs[1:]
    (x_ref, h0_ref, cst_ref, pst_ref, win_ref, cw_ref, cb_ref, gw_ref, gab_ref, gxb_ref,
     lam_ref, pw_ref, ps_ref, wout_ref, g_ref, b_ref,
     o_ref, ht_ref, cnew_ref, pnew_ref,
     ul_ext, up_ext, h_scr, a_scr, uu_scr, hs_scr) = refs
    c = pl.program_id(0)
    m = tt * nb
    hr = HIST * nb
    gw = GROUP_WIDTH
    vec = lambda ref: ref[layer:layer + 1, :]

    @pl.when(c == 0)
    def _():
        h_scr[...] = h0_ref[...]
        ul_ext[0:hr, :] = cst_ref[...]
        up_ext[0:hr, :] = pst_ref[...]

    x = x_ref[...]
    proj = _dot(x.astype(_BF16), win_ref[...])
    u_lru = proj[:, 0:gw]
    u_gate = proj[:, gw:2 * gw]
    u_pool = proj[:, 2 * gw:3 * gw]
    ul_ext[hr:hr + m, :] = u_lru
    up_ext[hr:hr + m, :] = u_pool

    xc = vec(cb_ref) + cw_ref[CONV_W - 1:CONV_W, :] * u_lru
    for k in range(CONV_W - 1):
        off = hr - (CONV_W - 1 - k) * nb
        xc = xc + cw_ref[k:k + 1, :] * ul_ext[off:off + m, :]

    xcb = xc.astype(_BF16)
    half = gw // 2
    ga, gx = [], []
    for p in range(2):
        gts = _dot(xcb[:, p * half:(p + 1) * half], gw_ref[p])
        ga.append(gts[:, 0:half])
        gx.append(gts[:, half:2 * half])
    r = jax.nn.sigmoid(jnp.concatenate(ga, axis=-1) + vec(gab_ref))
    i = jax.nn.sigmoid(jnp.concatenate(gx, axis=-1) + vec(gxb_ref))
    nl = -vec(lam_ref)
    softplus_nl = jnp.maximum(nl, 0.0) + jnp.log1p(jnp.exp(-jnp.abs(nl)))
    log_a = -LRU_C * r * softplus_nl
    a = jnp.exp(log_a)
    mult = jnp.sqrt(1.0 - a * a)
    a_scr[...] = a
    uu_scr[...] = mult * (i * xc)

    def step(t, h):
        rows = pl.ds(pl.multiple_of(t * nb, V7X_SUBLANES), nb)
        h = a_scr[rows, :] * h + uu_scr[rows, :]
        hs_scr[rows, :] = h
        return h

    h_last = lax.fori_loop(0, tt, step, h_scr[...], unroll=min(tt, 8))
    h_scr[...] = h_last
    ht_ref[...] = h_last
    y_lru = hs_scr[...] * jax.nn.gelu(u_gate)

    t_loc = lax.broadcasted_iota(jnp.int32, (tt, nb, V7X_LANES), 0).reshape(m, V7X_LANES)
    pos = start_pos + c * tt + t_loc
    pooled = []
    for gi, w in enumerate(POOL_WINDOWS):
        lanes = slice(gi * V7X_LANES, (gi + 1) * V7X_LANES)
        n_rows = (w - 1) * nb + m
        s = up_ext[hr - (w - 1) * nb:hr + m, lanes]
        span = 1
        while span < w:
            n_rows -= span * nb
            s = s[span * nb:span * nb + n_rows, :] + s[0:n_rows, :]
            span *= 2
        cnt = jnp.minimum(w, pos + 1).astype(_F32)
        pooled.append(s / cnt - u_pool[:, lanes])
    pooled = jnp.concatenate(pooled, axis=-1).astype(_BF16)
    y_pool = jnp.concatenate(
        [_dot(pooled[:, p * half:(p + 1) * half], pw_ref[p]) for p in range(2)],
        axis=-1) * vec(ps_ref)

    mix = _dot(jnp.concatenate([y_lru, y_pool], axis=-1).astype(_BF16), wout_ref[...])
    y = DEEPNORM_ALPHA * x + mix
    o_ref[...] = _layernorm(y, g_ref[1:2, :], b_ref[1:2, :])

    ul_tail = ul_ext[m:m + hr, :]
    up_tail = up_ext[m:m + hr, :]
    ul_ext[0:hr, :] = ul_tail
    up_ext[0:hr, :] = up_tail
    cnew_ref[...] = ul_tail
    pnew_ref[...] = up_tail


def _mixer(x, x_out, h0, cst, pst, win, cw, cb, gwt, gab, gxb, lam, pw, ps, wout, ln_g, ln_b,
           *, layer, nb, row0, n_rows, start_pos):
    t_total = n_rows // nb
    tt = min(MIXER_TIME_TILE, t_total)
    assert n_rows % nb == 0 and t_total % tt == 0 and nb % V7X_SUBLANES == 0
    m = tt * nb
    assert row0 % m == 0
    blk0 = row0 // m
    hr = HIST * nb
    gw = GROUP_WIDTH
    aliased = x_out is not None
    kern = functools.partial(_mixer_kernel, nb=nb, tt=tt, start_pos=start_pos, layer=layer,
                             aliased=aliased)
    stacked = lambda p: _layer_resident(p.shape, layer)
    in_specs = [pl.BlockSpec((m, D_MODEL), lambda i: (blk0 + i, 0)),
                _resident(h0.shape), _resident(cst.shape), _resident(pst.shape),
                stacked(win), stacked(cw), _resident(cb.shape), stacked(gwt),
                _resident(gab.shape), _resident(gxb.shape), _resident(lam.shape),
                stacked(pw), _resident(ps.shape), stacked(wout), stacked(ln_g), stacked(ln_b)]
    args = [x, h0, cst, pst, win, cw, cb, gwt, gab, gxb, lam, pw, ps, wout, ln_g, ln_b]
    if aliased:
        in_specs = [pl.BlockSpec(memory_space=pl.ANY)] + in_specs
        args = [x_out] + args
    return pl.pallas_call(
        kern,
        grid=(t_total // tt,),
        in_specs=in_specs,
        out_specs=[pl.BlockSpec((m, D_MODEL), lambda i: (blk0 + i, 0)),
                   pl.BlockSpec((nb, gw), lambda i: (0, 0)),
                   pl.BlockSpec((hr, gw), lambda i: (0, 0)),
                   pl.BlockSpec((hr, gw), lambda i: (0, 0))],
        out_shape=[jax.ShapeDtypeStruct(x.shape, _F32),
                   jax.ShapeDtypeStruct((nb, gw), _F32),
                   jax.ShapeDtypeStruct((hr, gw), _F32),
                   jax.ShapeDtypeStruct((hr, gw), _F32)],
        scratch_shapes=[pltpu.VMEM((hr + m, gw), _F32),
                        pltpu.VMEM((hr + m, gw), _F32),
                        pltpu.VMEM((nb, gw), _F32),
                        pltpu.VMEM((m, gw), _F32),
                        pltpu.VMEM((m, gw), _F32),
                        pltpu.VMEM((m, gw), _F32)],
        input_output_aliases={0: 0} if aliased else {},
        compiler_params=pltpu.CompilerParams(
            dimension_semantics=("arbitrary",),
            vmem_limit_bytes=V7X_VMEM_LIMIT_BYTES),
        name="mixer",
    )(*args)


def _block_diag_halves(w):
    depth, groups, n, _ = w.shape
    eye = jnp.eye(groups, dtype=w.dtype)
    full = (w[:, :, :, None, :] * eye[None, :, None, :, None]).reshape(
        depth, groups * n, groups * n)
    h = groups * n // 2
    return jnp.stack([full[:, :h, :h], full[:, h:, h:]], axis=1)


def _state_to_time_major(state):
    depth, nb, k, c = state.shape
    tm = jnp.transpose(state, (0, 2, 1, 3))
    tm = jnp.concatenate([jnp.zeros((depth, HIST - k, nb, c), state.dtype), tm], axis=1)
    return tm.reshape(depth, HIST * nb, c)


def _state_from_time_major(hist, nb, k):
    depth = hist.shape[0]
    tm = hist.reshape(depth, HIST, nb, hist.shape[-1])[:, HIST - k:]
    return jnp.transpose(tm, (0, 2, 1, 3))


def kernel(x_prompt, x_sample, state_lru_h, state_conv, state_pool, ln_g, ln_b,
           w1_gate, w1_up, w1_down, w_in, conv_w, conv_b, gate_a_w, gate_a_b,
           gate_x_w, gate_x_b, lru_lambda, pool_w, pool_scale, w_out,
           w2_gate, w2_up, w2_down):
    batch, seq, d = x_prompt.shape
    dec_batch = x_sample.shape[0]
    gw = GROUP_WIDTH
    n_prompt = seq * batch

    w1g, w1u, w1d = (w.astype(_BF16) for w in (w1_gate, w1_up, w1_down))
    w2g, w2u, w2d = (w.astype(_BF16) for w in (w2_gate, w2_up, w2_down))
    win = w_in.astype(_BF16)
    wout = w_out.astype(_BF16)
    gate_w = jnp.concatenate(
        [_block_diag_halves(gate_a_w), _block_diag_halves(gate_x_w)], axis=-1).astype(_BF16)
    pool_wb = _block_diag_halves(pool_w).astype(_BF16)
    mix_params = (win, conv_w, conv_b, gate_w, gate_a_b, gate_x_b, lru_lambda, pool_wb,
                  pool_scale, wout, ln_g, ln_b)

    x = jnp.concatenate(
        [jnp.transpose(x_prompt, (1, 0, 2)).reshape(n_prompt, d), x_sample.reshape(dec_batch, d)],
        axis=0)
    zeros_h = jnp.zeros((batch, gw), _F32)
    zeros_hist = jnp.zeros((HIST * batch, gw), _F32)
    s_conv_tm = _state_to_time_major(state_conv)
    s_pool_tm = _state_to_time_major(state_pool)

    p_h, p_conv, p_pool, s_h, s_conv, s_pool = [], [], [], [], [], []
    for l in range(DEPTH):
        x = _ffn(x, w1g, w1u, w1d, ln_g, ln_b, layer=l, ln_row=0)
        x_mid, h_t, cnew, pnew = _mixer(
            x, None, zeros_h, zeros_hist, zeros_hist, *mix_params,
            layer=l, nb=batch, row0=0, n_rows=n_prompt, start_pos=0)
        p_h.append(h_t); p_conv.append(cnew); p_pool.append(pnew)
        x, h_t, cnew, pnew = _mixer(
            x, x_mid, state_lru_h[l], s_conv_tm[l], s_pool_tm[l], *mix_params,
            layer=l, nb=dec_batch, row0=n_prompt, n_rows=dec_batch, start_pos=PAST_LEN)
        s_h.append(h_t); s_conv.append(cnew); s_pool.append(pnew)
        x = _ffn(x, w2g, w2u, w2d, ln_g, ln_b, layer=l, ln_row=2)

    y_prompt = jnp.transpose(x[:n_prompt].reshape(seq, batch, d), (1, 0, 2))
    y_sample = x[n_prompt:].reshape(dec_batch, 1, d)
    return (y_prompt, y_sample,
            jnp.stack(p_h),
            _state_from_time_major(jnp.stack(p_conv), batch, CONV_W - 1),
            _state_from_time_major(jnp.stack(p_pool), batch, POOL_MAX_WIN - 1),
            jnp.stack(s_h),
            _state_from_time_major(jnp.stack(s_conv), dec_batch, CONV_W - 1),
            _state_from_time_major(jnp.stack(s_pool), dec_batch, POOL_MAX_WIN - 1))
```

```python
import functools

import jax
import jax.numpy as jnp
from jax import lax
from jax.experimental import pallas as pl
from jax.experimental.pallas import tpu as pltpu

D_MODEL = 1024
DEPTH = 4
GROUP_WIDTH = 512
LRU_C = 8.0
CONV_W = 4
POOL_WINDOWS = (2, 4, 8, 16)
POOL_MAX_WIN = 16
D_FF = 2816
LN_EPS = 1e-5
DEEPNORM_ALPHA = (2.0 * DEPTH) ** 0.25
PAST_LEN = 16384

V7X_LANES = 128
V7X_SUBLANES = 8
V7X_MXU_DIM = 256
V7X_VMEM_LIMIT_BYTES = 56 * 1024 * 1024

HIST = POOL_MAX_WIN
FFN_TOKEN_TILE = 384
MIXER_TIME_TILE = 64
FFN_CHUNKS = (768, 768, 768, 512)
assert sum(FFN_CHUNKS) == D_FF and all(c % V7X_MXU_DIM == 0 for c in FFN_CHUNKS)

_BF16 = jnp.bfloat16
_F32 = jnp.float32


def _dot(a, b):
    return jnp.dot(a, b, preferred_element_type=_F32)


def _layernorm(y, g, b):
    mu = jnp.mean(y, axis=-1, keepdims=True)
    d = y - mu
    var = jnp.mean(d * d, axis=-1, keepdims=True)
    return d * lax.rsqrt(var + LN_EPS) * g + b


def _zero_tile_from(v, shape):
    u = lax.bitcast_convert_type(v, jnp.uint32)
    rows, cols = u.shape
    bf16_rows = 2 * V7X_SUBLANES
    acc = u[0:bf16_rows, :]
    for k in range(1, rows // bf16_rows):
        acc = acc | u[k * bf16_rows:(k + 1) * bf16_rows, :]
    word = acc[:, 0:V7X_LANES]
    for k in range(1, cols // V7X_LANES):
        word = word | acc[:, k * V7X_LANES:(k + 1) * V7X_LANES]
    zero = lax.bitcast_convert_type((word >> 16) >> 16, _F32).astype(_BF16)
    zero = jnp.concatenate([zero] * (shape[1] // V7X_LANES), axis=1)
    return jnp.concatenate([zero] * (shape[0] // bf16_rows), axis=0)


def _resident(shape):
    nd = len(shape)
    return pl.BlockSpec(shape, lambda *_: (0,) * nd, pipeline_mode=pl.Buffered(1))


def _layer_resident(shape, layer):
    nd = len(shape)
    return pl.BlockSpec((None,) + tuple(shape[1:]), lambda *_: (layer,) + (0,) * (nd - 1),
                        pipeline_mode=pl.Buffered(1))


def _ffn_kernel(x_ref, wg_ref, wu_ref, wd_ref, g_ref, b_ref, o_ref, y_scr, *, ln_row, n_tiles):
    i = pl.program_id(0)
    ln = lambda y: _layernorm(y, g_ref[ln_row:ln_row + 1, :], b_ref[ln_row:ln_row + 1, :])

    @pl.when(i == 0)
    def _():
        y_scr[...] = jnp.zeros(y_scr.shape, _F32)

    @pl.when(i < n_tiles)
    def _():
        x = x_ref[...]
        xb = x.astype(_BF16)
        tm = x.shape[0]
        n_parts = len(FFN_CHUNKS) - 1
        part = tm // n_parts
        acc = jnp.zeros(x.shape, _F32)
        lo = 0
        for ci, width in enumerate(FFN_CHUNKS):
            gate = _dot(xb, wg_ref[:, lo:lo + width])
            up = _dot(xb, wu_ref[:, lo:lo + width])
            hid = (jax.nn.silu(gate) * up).astype(_BF16)
            if ci > 0:
                rows = slice((ci - 1) * part, ci * part)
                normed = ln(y_scr[rows, :])
                o_ref[rows, :] = normed
                hid = hid + _zero_tile_from(normed, hid.shape)
            acc = acc + _dot(hid, wd_ref[lo:lo + width, :])
            lo += width
        y_scr[...] = DEEPNORM_ALPHA * x + 0.5 * acc

    @pl.when(i == n_tiles)
    def _():
        o_ref[...] = ln(y_scr[...])


def _ffn(x, wg, wu, wd, ln_g, ln_b, *, layer, ln_row):
    m = x.shape[0]
    tm = FFN_TOKEN_TILE
    assert m % tm == 0 and tm % ((len(FFN_CHUNKS) - 1) * 2 * V7X_SUBLANES) == 0
    n_tiles = m // tm
    return pl.pallas_call(
        functools.partial(_ffn_kernel, ln_row=ln_row, n_tiles=n_tiles),
        grid=(n_tiles + 1,),
        in_specs=[
            pl.BlockSpec((tm, D_MODEL), lambda i: (jnp.minimum(i, n_tiles - 1), 0)),
            _layer_resident(wg.shape, layer), _layer_resident(wu.shape, layer),
            _layer_resident(wd.shape, layer),
            _layer_resident(ln_g.shape, layer), _layer_resident(ln_b.shape, layer),
        ],
        out_specs=pl.BlockSpec((tm, D_MODEL), lambda i: (jnp.maximum(i - 1, 0), 0)),
        out_shape=jax.ShapeDtypeStruct((m, D_MODEL), _F32),
        scratch_shapes=[pltpu.VMEM((tm, D_MODEL), _F32)],
        compiler_params=pltpu.CompilerParams(
            dimension_semantics=("arbitrary",),
            vmem_limit_bytes=V7X_VMEM_LIMIT_BYTES),
        name="ffn",
    )(x, wg, wu, wd, ln_g, ln_b)


def _mixer_kernel(*refs, nb, tt, n_chunks, start_pos, layer, aliased):
    if aliased:
        refs = refs[1:]
    (x_ref, h0_ref, cst_ref, pst_ref, win_ref, cw_ref, cb_ref, gw_ref, gab_ref, gxb_ref,
     lam_ref, pw_ref, ps_ref, wout_ref, g_ref, b_ref,
     o_ref, ht_ref, cnew_ref, pnew_ref,
     ul_ext, up_ext, h_scr, a_scr, uu_scr, hs_scr, gate_scr, x_scr) = refs
    s = pl.program_id(0)
    pipelined = n_chunks > 1
    m = tt * nb
    hr = HIST * nb
    gw = GROUP_WIDTH
    half = gw // 2
    vec = lambda ref: ref[layer:layer + 1, :]

    @pl.when(s == 0)
    def _():
        h_scr[...] = h0_ref[...]
        ul_ext[0:hr, :] = cst_ref[...]
        if pipelined:
            a_scr[...] = jnp.ones(a_scr.shape, _F32)
            uu_scr[...] = jnp.zeros(uu_scr.shape, _F32)
            gate_scr[...] = jnp.zeros(gate_scr.shape, _F32)
            x_scr[...] = jnp.zeros(x_scr.shape, _F32)
            up_ext[0:m, :] = jnp.zeros((m, gw), _F32)
            up_ext[m:m + hr, :] = pst_ref[...]
        else:
            up_ext[0:hr, :] = pst_ref[...]

    def stage_a():
        x = x_ref[...]
        x_scr[...] = x
        proj = _dot(x.astype(_BF16), win_ref[...])
        u_lru = proj[:, 0:gw]
        gate_scr[...] = proj[:, gw:2 * gw]
        ul_ext[hr:hr + m, :] = u_lru
        up_ext[hr:hr + m, :] = proj[:, 2 * gw:3 * gw]

        xc = vec(cb_ref) + cw_ref[CONV_W - 1:CONV_W, :] * u_lru
        for k in range(CONV_W - 1):
            off = hr - (CONV_W - 1 - k) * nb
            xc = xc + cw_ref[k:k + 1, :] * ul_ext[off:off + m, :]

        xcb = xc.astype(_BF16)
        ga, gx = [], []
        for p in range(2):
            gts = _dot(xcb[:, p * half:(p + 1) * half], gw_ref[p])
            ga.append(gts[:, 0:half])
            gx.append(gts[:, half:2 * half])
        r = jax.nn.sigmoid(jnp.concatenate(ga, axis=-1) + vec(gab_ref))
        i = jax.nn.sigmoid(jnp.concatenate(gx, axis=-1) + vec(gxb_ref))
        nl = -vec(lam_ref)
        softplus_nl = jnp.maximum(nl, 0.0) + jnp.log1p(jnp.exp(-jnp.abs(nl)))
        log_a = -LRU_C * r * softplus_nl
        a = jnp.exp(log_a)
        mult = jnp.sqrt(1.0 - a * a)
        a_scr[...] = a
        uu_scr[...] = mult * (i * xc)
        ul_ext[0:hr, :] = ul_ext[m:m + hr, :]

    def stage_b(chunk):
        h = h_scr[...]
        for t in range(tt):
            rows = slice(t * nb, (t + 1) * nb)
            h = a_scr[rows, :] * h + uu_scr[rows, :]
            hs_scr[rows, :] = h
        h_scr[...] = h
        ht_ref[...] = h
        y_lru = hs_scr[...] * jax.nn.gelu(gate_scr[...])

        t_loc = lax.broadcasted_iota(jnp.int32, (tt, nb, V7X_LANES), 0).reshape(m, V7X_LANES)
        pos = start_pos + chunk * tt + t_loc
        pooled = []
        for gi, w in enumerate(POOL_WINDOWS):
            lanes = slice(gi * V7X_LANES, (gi + 1) * V7X_LANES)
            n_rows = (w - 1) * nb + m
            sums = up_ext[hr - (w - 1) * nb:hr + m, lanes]
            span = 1
            while span < w:
                n_rows -= span * nb
                sums = sums[span * nb:span * nb + n_rows, :] + sums[0:n_rows, :]
                span *= 2
            cnt = jnp.minimum(w, pos + 1).astype(_F32)
            pooled.append(sums / cnt - up_ext[hr:hr + m, lanes])
        pooled = jnp.concatenate(pooled, axis=-1).astype(_BF16)
        y_pool = jnp.concatenate(
            [_dot(pooled[:, p * half:(p + 1) * half], pw_ref[p]) for p in range(2)],
            axis=-1) * vec(ps_ref)

        mix = _dot(jnp.concatenate([y_lru, y_pool], axis=-1).astype(_BF16), wout_ref[...])
        y = DEEPNORM_ALPHA * x_scr[...] + mix
        o_ref[...] = _layernorm(y, g_ref[1:2, :], b_ref[1:2, :])
        up_tail = up_ext[m:m + hr, :]
        up_ext[0:hr, :] = up_tail
        pnew_ref[...] = up_tail

    if pipelined:
        stage_b(jnp.maximum(s - 1, 0))
        stage_a()
    else:
        stage_a()
        stage_b(s)

    @pl.when(s == n_chunks - 1)
    def _():
        cnew_ref[...] = ul_ext[0:hr, :]


def _mixer(x, x_out, h0, cst, pst, win, cw, cb, gwt, gab, gxb, lam, pw, ps, wout, ln_g, ln_b,
           *, layer, nb, row0, n_rows, start_pos):
    t_total = n_rows // nb
    tt = min(MIXER_TIME_TILE, t_total)
    assert n_rows % nb == 0 and t_total % tt == 0 and nb % V7X_SUBLANES == 0
    m = tt * nb
    assert row0 % m == 0
    blk0 = row0 // m
    n_chunks = t_total // tt
    n_steps = n_chunks + 1 if n_chunks > 1 else 1
    hr = HIST * nb
    gw = GROUP_WIDTH
    aliased = x_out is not None
    kern = functools.partial(_mixer_kernel, nb=nb, tt=tt, n_chunks=n_chunks, start_pos=start_pos,
                             layer=layer, aliased=aliased)
    stacked = lambda p: _layer_resident(p.shape, layer)
    in_specs = [pl.BlockSpec((m, D_MODEL), lambda i: (blk0 + jnp.minimum(i, n_chunks - 1), 0)),
                _resident(h0.shape), _resident(cst.shape), _resident(pst.shape),
                stacked(win), stacked(cw), _resident(cb.shape), stacked(gwt),
                _resident(gab.shape), _resident(gxb.shape), _resident(lam.shape),
                stacked(pw), _resident(ps.shape), stacked(wout), stacked(ln_g), stacked(ln_b)]
    args = [x, h0, cst, pst, win, cw, cb, gwt, gab, gxb, lam, pw, ps, wout, ln_g, ln_b]
    if aliased:
        in_specs = [pl.BlockSpec(memory_space=pl.ANY)] + in_specs
        args = [x_out] + args
    out_blk = (lambda i: (blk0 + jnp.maximum(i - 1, 0), 0)) if n_chunks > 1 else (lambda i: (blk0, 0))
    return pl.pallas_call(
        kern,
        grid=(n_steps,),
        in_specs=in_specs,
        out_specs=[pl.BlockSpec((m, D_MODEL), out_blk),
                   pl.BlockSpec((nb, gw), lambda i: (0, 0)),
                   pl.BlockSpec((hr, gw), lambda i: (0, 0)),
                   pl.BlockSpec((hr, gw), lambda i: (0, 0))],
        out_shape=[jax.ShapeDtypeStruct(x.shape, _F32),
                   jax.ShapeDtypeStruct((nb, gw), _F32),
                   jax.ShapeDtypeStruct((hr, gw), _F32),
                   jax.ShapeDtypeStruct((hr, gw), _F32)],
        scratch_shapes=[pltpu.VMEM((hr + m, gw), _F32),
                        pltpu.VMEM((hr + m, gw), _F32),
                        pltpu.VMEM((nb, gw), _F32),
                        pltpu.VMEM((m, gw), _F32),
                        pltpu.VMEM((m, gw), _F32),
                        pltpu.VMEM((m, gw), _F32),
                        pltpu.VMEM((m, gw), _F32),
                        pltpu.VMEM((m, D_MODEL), _F32)],
        input_output_aliases={0: 0} if aliased else {},
        compiler_params=pltpu.CompilerParams(
            dimension_semantics=("arbitrary",),
            vmem_limit_bytes=V7X_VMEM_LIMIT_BYTES),
        name="mixer",
    )(*args)


def _block_diag_halves(w):
    depth, groups, n, _ = w.shape
    eye = jnp.eye(groups, dtype=w.dtype)
    full = (w[:, :, :, None, :] * eye[None, :, None, :, None]).reshape(
        depth, groups * n, groups * n)
    h = groups * n // 2
    return jnp.stack([full[:, :h, :h], full[:, h:, h:]], axis=1)


def _state_to_time_major(state):
    depth, nb, k, c = state.shape
    tm = jnp.transpose(state, (0, 2, 1, 3))
    tm = jnp.concatenate([jnp.zeros((depth, HIST - k, nb, c), state.dtype), tm], axis=1)
    return tm.reshape(depth, HIST * nb, c)


def _state_from_time_major(hist, nb, k):
    depth = hist.shape[0]
    tm = hist.reshape(depth, HIST, nb, hist.shape[-1])[:, HIST - k:]
    return jnp.transpose(tm, (0, 2, 1, 3))


def kernel(x_prompt, x_sample, state_lru_h, state_conv, state_pool, ln_g, ln_b,
           w1_gate, w1_up, w1_down, w_in, conv_w, conv_b, gate_a_w, gate_a_b,
           gate_x_w, gate_x_b, lru_lambda, pool_w, pool_scale, w_out,
           w2_gate, w2_up, w2_down):
    batch, seq, d = x_prompt.shape
    dec_batch = x_sample.shape[0]
    gw = GROUP_WIDTH
    n_prompt = seq * batch

    w1g, w1u, w1d = (w.astype(_BF16) for w in (w1_gate, w1_up, w1_down))
    w2g, w2u, w2d = (w.astype(_BF16) for w in (w2_gate, w2_up, w2_down))
    win = w_in.astype(_BF16)
    wout = w_out.astype(_BF16)
    gate_w = jnp.concatenate(
        [_block_diag_halves(gate_a_w), _block_diag_halves(gate_x_w)], axis=-1).astype(_BF16)
    pool_wb = _block_diag_halves(pool_w).astype(_BF16)
    mix_params = (win, conv_w, conv_b, gate_w, gate_a_b, gate_x_b, lru_lambda, pool_wb,
                  pool_scale, wout, ln_g, ln_b)

    x = jnp.concatenate(
        [jnp.transpose(x_prompt, (1, 0, 2)).reshape(n_prompt, d), x_sample.reshape(dec_batch, d)],
        axis=0)
    zeros_h = jnp.zeros((batch, gw), _F32)
    zeros_hist = jnp.zeros((HIST * batch, gw), _F32)
    s_conv_tm = _state_to_time_major(state_conv)
    s_pool_tm = _state_to_time_major(state_pool)

    p_h, p_conv, p_pool, s_h, s_conv, s_pool = [], [], [], [], [], []
    for l in range(DEPTH):
        x = _ffn(x, w1g, w1u, w1d, ln_g, ln_b, layer=l, ln_row=0)
        x_mid, h_t, cnew, pnew = _mixer(
            x, None, zeros_h, zeros_hist, zeros_hist, *mix_params,
            layer=l, nb=batch, row0=0, n_rows=n_prompt, start_pos=0)
        p_h.append(h_t); p_conv.append(cnew); p_pool.append(pnew)
        x, h_t, cnew, pnew = _mixer(
            x, x_mid, state_lru_h[l], s_conv_tm[l], s_pool_tm[l], *mix_params,
            layer=l, nb=dec_batch, row0=n_prompt, n_rows=dec_batch, start_pos=PAST_LEN)
        s_h.append(h_t); s_conv.append(cnew); s_pool.append(pnew)
        x = _ffn(x, w2g, w2u, w2d, ln_g, ln_b, layer=l, ln_row=2)

    y_prompt = jnp.transpose(x[:n_prompt].reshape(seq, batch, d), (1, 0, 2))
    y_sample = x[n_prompt:].reshape(dec_batch, 1, d)
    return (y_prompt, y_sample,
            jnp.stack(p_h),
            _state_from_time_major(jnp.stack(p_conv), batch, CONV_W - 1),
            _state_from_time_major(jnp.stack(p_pool), batch, POOL_MAX_WIN - 1),
            jnp.stack(s_h),
            _state_from_time_major(jnp.stack(s_conv), dec_batch, CONV_W - 1),
            _state_from_time_major(jnp.stack(s_pool), dec_batch, POOL_MAX_WIN - 1))
```

```python
import functools

import jax
import jax.numpy as jnp
from jax import lax
from jax.experimental import pallas as pl
from jax.experimental.pallas import tpu as pltpu

D_MODEL = 1024
DEPTH = 4
GROUP_WIDTH = 512
LRU_C = 8.0
CONV_W = 4
POOL_WINDOWS = (2, 4, 8, 16)
POOL_MAX_WIN = 16
D_FF = 2816
LN_EPS = 1e-5
DEEPNORM_ALPHA = (2.0 * DEPTH) ** 0.25
PAST_LEN = 16384

V7X_LANES = 128
V7X_SUBLANES = 8
V7X_MXU_DIM = 256
V7X_VMEM_LIMIT_BYTES = 56 * 1024 * 1024

HIST = POOL_MAX_WIN
FFN_TOKEN_TILE = 512
MIXER_TIME_TILE = 64
FFN_CHUNKS = (768, 768, 768, 512)
assert sum(FFN_CHUNKS) == D_FF and all(c % V7X_MXU_DIM == 0 for c in FFN_CHUNKS)
FFN_LOAD_COLS = V7X_MXU_DIM
assert D_FF % FFN_LOAD_COLS == 0

_BF16 = jnp.bfloat16
_F32 = jnp.float32


def _dot(a, b):
    return jnp.dot(a, b, preferred_element_type=_F32)


def _layernorm(y, g, b):
    mu = jnp.mean(y, axis=-1, keepdims=True)
    d = y - mu
    var = jnp.mean(d * d, axis=-1, keepdims=True)
    return d * lax.rsqrt(var + LN_EPS) * g + b


def _zero_tile_from(v, shape):
    u = lax.bitcast_convert_type(v, jnp.uint32)
    rows, cols = u.shape
    bf16_rows = 2 * V7X_SUBLANES
    acc = u[0:bf16_rows, :]
    for k in range(1, rows // bf16_rows):
        acc = acc | u[k * bf16_rows:(k + 1) * bf16_rows, :]
    word = acc[:, 0:V7X_LANES]
    for k in range(1, cols // V7X_LANES):
        word = word | acc[:, k * V7X_LANES:(k + 1) * V7X_LANES]
    zero = lax.bitcast_convert_type((word >> 16) >> 16, _F32).astype(_BF16)
    zero = jnp.concatenate([zero] * (shape[1] // V7X_LANES), axis=1)
    return jnp.concatenate([zero] * (shape[0] // bf16_rows), axis=0)


def _resident(shape):
    nd = len(shape)
    return pl.BlockSpec(shape, lambda *_: (0,) * nd, pipeline_mode=pl.Buffered(1))


def _layer_resident(shape, layer):
    nd = len(shape)
    return pl.BlockSpec((None,) + tuple(shape[1:]), lambda *_: (layer,) + (0,) * (nd - 1),
                        pipeline_mode=pl.Buffered(1))


def _load_ffn_weights(layer, wg_hbm, wu_hbm, wd_hbm, wg_v, wu_v, wd_v, st_gu, st_d, sems):
    n_steps = D_FF // FFN_LOAD_COLS

    def copies(c, slot):
        cols = pl.ds(c * FFN_LOAD_COLS, FFN_LOAD_COLS)
        return (pltpu.make_async_copy(wg_hbm.at[layer, :, cols], st_gu.at[slot, 0], sems.at[slot, 0]),
                pltpu.make_async_copy(wu_hbm.at[layer, :, cols], st_gu.at[slot, 1], sems.at[slot, 1]),
                pltpu.make_async_copy(wd_hbm.at[layer, cols, :], st_d.at[slot], sems.at[slot, 2]))

    for cp in copies(0, 0):
        cp.start()
    for c in range(n_steps):
        slot = c % 2
        if c + 1 < n_steps:
            for cp in copies(c + 1, 1 - slot):
                cp.start()
        for cp in copies(c, slot):
            cp.wait()
        lo = c * FFN_LOAD_COLS
        wg_v[:, lo:lo + FFN_LOAD_COLS] = st_gu[slot, 0].astype(_BF16)
        wu_v[:, lo:lo + FFN_LOAD_COLS] = st_gu[slot, 1].astype(_BF16)
        wd_v[lo:lo + FFN_LOAD_COLS, :] = st_d[slot].astype(_BF16)


def _swiglu(xb, wg_v, wu_v, wd_v, per_chunk=None):
    acc = jnp.zeros((xb.shape[0], D_MODEL), _F32)
    lo = 0
    for ci, width in enumerate(FFN_CHUNKS):
        gate = _dot(xb, wg_v[:, lo:lo + width])
        up = _dot(xb, wu_v[:, lo:lo + width])
        hid = (jax.nn.silu(gate) * up).astype(_BF16)
        if per_chunk is not None:
            hid = per_chunk(ci, hid)
        acc = acc + _dot(hid, wd_v[lo:lo + width, :])
        lo += width
    return acc


def _ffn_kernel(xp_ref, xs_ref, wg_hbm, wu_hbm, wd_hbm, g_ref, b_ref, op_ref, os_ref,
                wg_v, wu_v, wd_v, st_gu, st_d, sems, y_scr, ys_scr, *, layer, ln_row, n_tiles):
    s = pl.program_id(0)
    ln = lambda y: _layernorm(y, g_ref[ln_row:ln_row + 1, :], b_ref[ln_row:ln_row + 1, :])

    @pl.when(s == 0)
    def _():
        _load_ffn_weights(layer, wg_hbm, wu_hbm, wd_hbm, wg_v, wu_v, wd_v, st_gu, st_d, sems)
        y_scr[...] = jnp.zeros(y_scr.shape, _F32)

    @pl.when(s < n_tiles)
    def _():
        x = xp_ref[...]
        part = x.shape[0] // len(FFN_CHUNKS)

        def ln_part(ci, hid):
            rows = slice(ci * part, (ci + 1) * part)
            normed = ln(y_scr[rows, :])
            op_ref[rows, :] = normed
            return hid + _zero_tile_from(normed, hid.shape)

        acc = _swiglu(x.astype(_BF16), wg_v, wu_v, wd_v, ln_part)
        y_scr[...] = DEEPNORM_ALPHA * x + 0.5 * acc

    @pl.when(s == n_tiles)
    def _():
        op_ref[...] = ln(y_scr[...])
        xs = xs_ref[...]
        ys_scr[...] = DEEPNORM_ALPHA * xs + 0.5 * _swiglu(xs.astype(_BF16), wg_v, wu_v, wd_v)

    @pl.when(s == n_tiles + 1)
    def _():
        os_ref[...] = ln(ys_scr[...])


def _ffn(xp, xs, wg, wu, wd, ln_g, ln_b, *, layer, ln_row):
    m, ms = xp.shape[0], xs.shape[0]
    tm = FFN_TOKEN_TILE
    assert m % tm == 0 and tm % (len(FFN_CHUNKS) * 2 * V7X_SUBLANES) == 0
    n_tiles = m // tm
    hbm = pl.BlockSpec(memory_space=pl.ANY)
    return pl.pallas_call(
        functools.partial(_ffn_kernel, layer=layer, ln_row=ln_row, n_tiles=n_tiles),
        grid=(n_tiles + 2,),
        in_specs=[
            pl.BlockSpec((tm, D_MODEL), lambda i: (jnp.minimum(i, n_tiles - 1), 0)),
            pl.BlockSpec((ms, D_MODEL), lambda i: (0, 0)),
            hbm, hbm, hbm,
            _layer_resident(ln_g.shape, layer), _layer_resident(ln_b.shape, layer),
        ],
        out_specs=[pl.BlockSpec((tm, D_MODEL), lambda i: (jnp.clip(i - 1, 0, n_tiles - 1), 0)),
                   pl.BlockSpec((ms, D_MODEL), lambda i: (0, 0))],
        out_shape=[jax.ShapeDtypeStruct((m, D_MODEL), _F32),
                   jax.ShapeDtypeStruct((ms, D_MODEL), _F32)],
        scratch_shapes=[pltpu.VMEM((D_MODEL, D_FF), _BF16),
                        pltpu.VMEM((D_MODEL, D_FF), _BF16),
                        pltpu.VMEM((D_FF, D_MODEL), _BF16),
                        pltpu.VMEM((2, 2, D_MODEL, FFN_LOAD_COLS), _F32),
                        pltpu.VMEM((2, FFN_LOAD_COLS, D_MODEL), _F32),
                        pltpu.SemaphoreType.DMA((2, 3)),
                        pltpu.VMEM((tm, D_MODEL), _F32),
                        pltpu.VMEM((ms, D_MODEL), _F32)],
        compiler_params=pltpu.CompilerParams(
            dimension_semantics=("arbitrary",),
            vmem_limit_bytes=V7X_VMEM_LIMIT_BYTES),
        name="ffn",
    )(xp, xs, wg, wu, wd, ln_g, ln_b)


def _mixer_kernel(x_ref, h0_ref, cst_ref, pst_ref, win_ref, cw_ref, cb_ref, gw_ref, gab_ref,
                  gxb_ref, lam_ref, pw_ref, ps_ref, wout_ref, g_ref, b_ref,
                  o_ref, ht_ref, cnew_ref, pnew_ref,
                  ul_ext, up_ext, h_scr, a_scr, uu_scr, hs_scr, gate_scr, x_scr,
                  *, nb, tt, n_chunks, start_pos, layer):
    s = pl.program_id(0)
    pipelined = n_chunks > 1
    m = tt * nb
    hr = HIST * nb
    gw = GROUP_WIDTH
    half = gw // 2
    vec = lambda ref: ref[layer:layer + 1, :]

    @pl.when(s == 0)
    def _():
        h_scr[...] = h0_ref[...]
        ul_ext[0:hr, :] = cst_ref[...]
        if pipelined:
            a_scr[...] = jnp.ones(a_scr.shape, _F32)
            uu_scr[...] = jnp.zeros(uu_scr.shape, _F32)
            gate_scr[...] = jnp.zeros(gate_scr.shape, _F32)
            x_scr[...] = jnp.zeros(x_scr.shape, _F32)
            up_ext[0:m, :] = jnp.zeros((m, gw), _F32)
            up_ext[m:m + hr, :] = pst_ref[...]
        else:
            up_ext[0:hr, :] = pst_ref[...]

    def stage_a():
        x = x_ref[...]
        x_scr[...] = x
        proj = _dot(x.astype(_BF16), win_ref[...])
        u_lru = proj[:, 0:gw]
        gate_scr[...] = proj[:, gw:2 * gw]
        ul_ext[hr:hr + m, :] = u_lru
        up_ext[hr:hr + m, :] = proj[:, 2 * gw:3 * gw]

        xc = vec(cb_ref) + cw_ref[CONV_W - 1:CONV_W, :] * u_lru
        for k in range(CONV_W - 1):
            off = hr - (CONV_W - 1 - k) * nb
            xc = xc + cw_ref[k:k + 1, :] * ul_ext[off:off + m, :]

        xcb = xc.astype(_BF16)
        ga, gx = [], []
        for p in range(2):
            gts = _dot(xcb[:, p * half:(p + 1) * half], gw_ref[p])
            ga.append(gts[:, 0:half])
            gx.append(gts[:, half:2 * half])
        r = jax.nn.sigmoid(jnp.concatenate(ga, axis=-1) + vec(gab_ref))
        i = jax.nn.sigmoid(jnp.concatenate(gx, axis=-1) + vec(gxb_ref))
        nl = -vec(lam_ref)
        softplus_nl = jnp.maximum(nl, 0.0) + jnp.log1p(jnp.exp(-jnp.abs(nl)))
        log_a = -LRU_C * r * softplus_nl
        a = jnp.exp(log_a)
        mult = jnp.sqrt(1.0 - a * a)
        a_scr[...] = a
        uu_scr[...] = mult * (i * xc)
        ul_ext[0:hr, :] = ul_ext[m:m + hr, :]

    def stage_b(chunk):
        h = h_scr[...]
        for t in range(tt):
            rows = slice(t * nb, (t + 1) * nb)
            h = a_scr[rows, :] * h + uu_scr[rows, :]
            hs_scr[rows, :] = h
        h_scr[...] = h
        ht_ref[...] = h
        y_lru = hs_scr[...] * jax.nn.gelu(gate_scr[...])

        t_loc = lax.broadcasted_iota(jnp.int32, (tt, nb, V7X_LANES), 0).reshape(m, V7X_LANES)
        pos = start_pos + chunk * tt + t_loc
        pooled = []
        for gi, w in enumerate(POOL_WINDOWS):
            lanes = slice(gi * V7X_LANES, (gi + 1) * V7X_LANES)
            n_rows = (w - 1) * nb + m
            sums = up_ext[hr - (w - 1) * nb:hr + m, lanes]
            span = 1
            while span < w:
                n_rows -= span * nb
                sums = sums[span * nb:span * nb + n_rows, :] + sums[0:n_rows, :]
                span *= 2
            cnt = jnp.minimum(w, pos + 1).astype(_F32)
            pooled.append(sums / cnt - up_ext[hr:hr + m, lanes])
        pooled = jnp.concatenate(pooled, axis=-1).astype(_BF16)
        y_pool = jnp.concatenate(
            [_dot(pooled[:, p * half:(p + 1) * half], pw_ref[p]) for p in range(2)],
            axis=-1) * vec(ps_ref)

        mix = _dot(jnp.concatenate([y_lru, y_pool], axis=-1).astype(_BF16), wout_ref[...])
        y = DEEPNORM_ALPHA * x_scr[...] + mix
        o_ref[...] = _layernorm(y, g_ref[1:2, :], b_ref[1:2, :])
        up_tail = up_ext[m:m + hr, :]
        up_ext[0:hr, :] = up_tail
        pnew_ref[...] = up_tail

    if pipelined:
        stage_b(jnp.maximum(s - 1, 0))
        stage_a()
    else:
        stage_a()
        stage_b(s)

    @pl.when(s == n_chunks - 1)
    def _():
        cnew_ref[...] = ul_ext[0:hr, :]


def _mixer(x, h0, cst, pst, win, cw, cb, gwt, gab, gxb, lam, pw, ps, wout, ln_g, ln_b,
           *, layer, nb, start_pos):
    n_rows = x.shape[0]
    t_total = n_rows // nb
    tt = min(MIXER_TIME_TILE, t_total)
    assert n_rows % nb == 0 and t_total % tt == 0 and nb % V7X_SUBLANES == 0
    m = tt * nb
    n_chunks = t_total // tt
    n_steps = n_chunks + 1 if n_chunks > 1 else 1
    hr = HIST * nb
    gw = GROUP_WIDTH
    kern = functools.partial(_mixer_kernel, nb=nb, tt=tt, n_chunks=n_chunks, start_pos=start_pos,
                             layer=layer)
    stacked = lambda p: _layer_resident(p.shape, layer)
    out_blk = (lambda i: (jnp.maximum(i - 1, 0), 0)) if n_chunks > 1 else (lambda i: (0, 0))
    return pl.pallas_call(
        kern,
        grid=(n_steps,),
        in_specs=[pl.BlockSpec((m, D_MODEL), lambda i: (jnp.minimum(i, n_chunks - 1), 0)),
                  _resident(h0.shape), _resident(cst.shape), _resident(pst.shape),
                  stacked(win), stacked(cw), _resident(cb.shape), stacked(gwt),
                  _resident(gab.shape), _resident(gxb.shape), _resident(lam.shape),
                  stacked(pw), _resident(ps.shape), stacked(wout), stacked(ln_g), stacked(ln_b)],
        out_specs=[pl.BlockSpec((m, D_MODEL), out_blk),
                   pl.BlockSpec((nb, gw), lambda i: (0, 0)),
                   pl.BlockSpec((hr, gw), lambda i: (0, 0)),
                   pl.BlockSpec((hr, gw), lambda i: (0, 0))],
        out_shape=[jax.ShapeDtypeStruct(x.shape, _F32),
                   jax.ShapeDtypeStruct((nb, gw), _F32),
                   jax.ShapeDtypeStruct((hr, gw), _F32),
                   jax.ShapeDtypeStruct((hr, gw), _F32)],
        scratch_shapes=[pltpu.VMEM((hr + m, gw), _F32),
                        pltpu.VMEM((hr + m, gw), _F32),
                        pltpu.VMEM((nb, gw), _F32),
                        pltpu.VMEM((m, gw), _F32),
                        pltpu.VMEM((m, gw), _F32),
                        pltpu.VMEM((m, gw), _F32),
                        pltpu.VMEM((m, gw), _F32),
                        pltpu.VMEM((m, D_MODEL), _F32)],
        compiler_params=pltpu.CompilerParams(
            dimension_semantics=("arbitrary",),
            vmem_limit_bytes=V7X_VMEM_LIMIT_BYTES),
        name="mixer",
    )(x, h0, cst, pst, win, cw, cb, gwt, gab, gxb, lam, pw, ps, wout, ln_g, ln_b)


def _block_diag_halves(w):
    depth, groups, n, _ = w.shape
    eye = jnp.eye(groups, dtype=w.dtype)
    full = (w[:, :, :, None, :] * eye[None, :, None, :, None]).reshape(
        depth, groups * n, groups * n)
    h = groups * n // 2
    return jnp.stack([full[:, :h, :h], full[:, h:, h:]], axis=1)


def _state_to_time_major(state):
    depth, nb, k, c = state.shape
    tm = jnp.transpose(state, (0, 2, 1, 3))
    tm = jnp.concatenate([jnp.zeros((depth, HIST - k, nb, c), state.dtype), tm], axis=1)
    return tm.reshape(depth, HIST * nb, c)


def _state_from_time_major(hist, nb, k):
    depth = hist.shape[0]
    tm = hist.reshape(depth, HIST, nb, hist.shape[-1])[:, HIST - k:]
    return jnp.transpose(tm, (0, 2, 1, 3))


def kernel(x_prompt, x_sample, state_lru_h, state_conv, state_pool, ln_g, ln_b,
           w1_gate, w1_up, w1_down, w_in, conv_w, conv_b, gate_a_w, gate_a_b,
           gate_x_w, gate_x_b, lru_lambda, pool_w, pool_scale, w_out,
           w2_gate, w2_up, w2_down):
    batch, seq, d = x_prompt.shape
    dec_batch = x_sample.shape[0]
    gw = GROUP_WIDTH

    win = w_in.astype(_BF16)
    wout = w_out.astype(_BF16)
    gate_w = jnp.concatenate(
        [_block_diag_halves(gate_a_w), _block_diag_halves(gate_x_w)], axis=-1).astype(_BF16)
    pool_wb = _block_diag_halves(pool_w).astype(_BF16)
    mix_params = (win, conv_w, conv_b, gate_w, gate_a_b, gate_x_b, lru_lambda, pool_wb,
                  pool_scale, wout, ln_g, ln_b)

    xp = jnp.transpose(x_prompt, (1, 0, 2)).reshape(seq * batch, d)
    xs = x_sample.reshape(dec_batch, d)
    zeros_h = jnp.zeros((batch, gw), _F32)
    zeros_hist = jnp.zeros((HIST * batch, gw), _F32)
    s_conv_tm = _state_to_time_major(state_conv)
    s_pool_tm = _state_to_time_major(state_pool)

    p_h, p_conv, p_pool, s_h, s_conv, s_pool = [], [], [], [], [], []
    for l in range(DEPTH):
        xp, xs = _ffn(xp, xs, w1_gate, w1_up, w1_down, ln_g, ln_b, layer=l, ln_row=0)
        xp, h_t, cnew, pnew = _mixer(xp, zeros_h, zeros_hist, zeros_hist, *mix_params,
                                     layer=l, nb=batch, start_pos=0)
        p_h.append(h_t); p_conv.append(cnew); p_pool.append(pnew)
        xs, h_t, cnew, pnew = _mixer(xs, state_lru_h[l], s_conv_tm[l], s_pool_tm[l], *mix_params,
                                     layer=l, nb=dec_batch, start_pos=PAST_LEN)
        s_h.append(h_t); s_conv.append(cnew); s_pool.append(pnew)
        xp, xs = _ffn(xp, xs, w2_gate, w2_up, w2_down, ln_g, ln_b, layer=l, ln_row=2)

    y_prompt = jnp.transpose(xp.reshape(seq, batch, d), (1, 0, 2))
    y_sample = xs.reshape(dec_batch, 1, d)
    return (y_prompt, y_sample,
            jnp.stack(p_h),
            _state_from_time_major(jnp.stack(p_conv), batch, CONV_W - 1),
            _state_from_time_major(jnp.stack(p_pool), batch, POOL_MAX_WIN - 1),
            jnp.stack(s_h),
            _state_from_time_major(jnp.stack(s_conv), dec_batch, CONV_W - 1),
            _state_from_time_major(jnp.stack(s_pool), dec_batch, POOL_MAX_WIN - 1))
```

```python
import functools

import jax
import jax.numpy as jnp
from jax import lax
from jax.experimental import pallas as pl
from jax.experimental.pallas import tpu as pltpu

D_MODEL = 1024
DEPTH = 4
GROUP_WIDTH = 512
LRU_C = 8.0
CONV_W = 4
POOL_WINDOWS = (2, 4, 8, 16)
POOL_MAX_WIN = 16
D_FF = 2816
LN_EPS = 1e-5
DEEPNORM_ALPHA = (2.0 * DEPTH) ** 0.25
PAST_LEN = 16384

V7X_LANES = 128
V7X_SUBLANES = 8
V7X_MXU_DIM = 256
V7X_VMEM_LIMIT_BYTES = 56 * 1024 * 1024

HIST = POOL_MAX_WIN
FFN_TOKEN_TILE = 512
MIXER_TIME_TILE = 64
FFN_CHUNKS = (768, 768, 768, 512)
assert sum(FFN_CHUNKS) == D_FF and all(c % V7X_MXU_DIM == 0 for c in FFN_CHUNKS)
FFN_LOAD_COLS = V7X_MXU_DIM
FFN_LOAD_SLOTS = 3
assert D_FF % FFN_LOAD_COLS == 0 and all(c % FFN_LOAD_COLS == 0 for c in FFN_CHUNKS)

_BF16 = jnp.bfloat16
_F32 = jnp.float32


def _dot(a, b):
    return jnp.dot(a, b, preferred_element_type=_F32)


def _layernorm(y, g, b):
    mu = jnp.mean(y, axis=-1, keepdims=True)
    d = y - mu
    var = jnp.mean(d * d, axis=-1, keepdims=True)
    return d * lax.rsqrt(var + LN_EPS) * g + b


def _zero_tile_from(v, shape):
    u = lax.bitcast_convert_type(v, jnp.uint32)
    rows, cols = u.shape
    bf16_rows = 2 * V7X_SUBLANES
    acc = u[0:bf16_rows, :]
    for k in range(1, rows // bf16_rows):
        acc = acc | u[k * bf16_rows:(k + 1) * bf16_rows, :]
    word = acc[:, 0:V7X_LANES]
    for k in range(1, cols // V7X_LANES):
        word = word | acc[:, k * V7X_LANES:(k + 1) * V7X_LANES]
    zero = lax.bitcast_convert_type((word >> 16) >> 16, _F32).astype(_BF16)
    zero = jnp.concatenate([zero] * (shape[1] // V7X_LANES), axis=1)
    return jnp.concatenate([zero] * (shape[0] // bf16_rows), axis=0)


def _resident(shape):
    nd = len(shape)
    return pl.BlockSpec(shape, lambda *_: (0,) * nd, pipeline_mode=pl.Buffered(1))


def _layer_resident(shape, layer):
    nd = len(shape)
    return pl.BlockSpec((None,) + tuple(shape[1:]), lambda *_: (layer,) + (0,) * (nd - 1),
                        pipeline_mode=pl.Buffered(1))


def _ffn_weight_loader(layer, wg_hbm, wu_hbm, wd_hbm, wg_v, wu_v, wd_v, st_gu, st_d, sems):
    n_pieces = D_FF // FFN_LOAD_COLS

    def copies(p):
        slot = p % FFN_LOAD_SLOTS
        cols = pl.ds(p * FFN_LOAD_COLS, FFN_LOAD_COLS)
        return (pltpu.make_async_copy(wg_hbm.at[layer, :, cols], st_gu.at[slot, 0], sems.at[slot, 0]),
                pltpu.make_async_copy(wu_hbm.at[layer, :, cols], st_gu.at[slot, 1], sems.at[slot, 1]),
                pltpu.make_async_copy(wd_hbm.at[layer, cols, :], st_d.at[slot], sems.at[slot, 2]))

    def start(p):
        if p < n_pieces:
            for cp in copies(p):
                cp.start()

    def finish(p):
        slot = p % FFN_LOAD_SLOTS
        for cp in copies(p):
            cp.wait()
        lo = p * FFN_LOAD_COLS
        wg_v[:, lo:lo + FFN_LOAD_COLS] = st_gu[slot, 0].astype(_BF16)
        wu_v[:, lo:lo + FFN_LOAD_COLS] = st_gu[slot, 1].astype(_BF16)
        wd_v[lo:lo + FFN_LOAD_COLS, :] = st_d[slot].astype(_BF16)

    return start, finish


def _swiglu(xb, wg_v, wu_v, wd_v, per_chunk=None, before_chunk=None):
    acc = jnp.zeros((xb.shape[0], D_MODEL), _F32)
    lo = 0
    for ci, width in enumerate(FFN_CHUNKS):
        if before_chunk is not None:
            before_chunk(lo, width)
        gate = _dot(xb, wg_v[:, lo:lo + width])
        up = _dot(xb, wu_v[:, lo:lo + width])
        hid = (jax.nn.silu(gate) * up).astype(_BF16)
        if per_chunk is not None:
            hid = per_chunk(ci, hid)
        acc = acc + _dot(hid, wd_v[lo:lo + width, :])
        lo += width
    return acc


def _ffn_kernel(xp_ref, xs_ref, wg_hbm, wu_hbm, wd_hbm, g_ref, b_ref, op_ref, os_ref,
                wg_v, wu_v, wd_v, st_gu, st_d, sems, y_scr, ys_scr, *, layer, ln_row, n_tiles):
    s = pl.program_id(0)
    ln = lambda y: _layernorm(y, g_ref[ln_row:ln_row + 1, :], b_ref[ln_row:ln_row + 1, :])

    def prompt_tile(before_chunk=None):
        x = xp_ref[...]
        part = x.shape[0] // len(FFN_CHUNKS)

        def ln_part(ci, hid):
            rows = slice(ci * part, (ci + 1) * part)
            normed = ln(y_scr[rows, :])
            op_ref[rows, :] = normed
            return hid + _zero_tile_from(normed, hid.shape)

        acc = _swiglu(x.astype(_BF16), wg_v, wu_v, wd_v, ln_part, before_chunk)
        y_scr[...] = DEEPNORM_ALPHA * x + 0.5 * acc

    @pl.when(s == 0)
    def _():
        y_scr[...] = jnp.zeros(y_scr.shape, _F32)
        start, finish = _ffn_weight_loader(layer, wg_hbm, wu_hbm, wd_hbm, wg_v, wu_v, wd_v,
                                           st_gu, st_d, sems)
        for p in range(FFN_LOAD_SLOTS):
            start(p)

        def load_chunk(lo, width):
            for p in range(lo // FFN_LOAD_COLS, (lo + width) // FFN_LOAD_COLS):
                finish(p)
                start(p + FFN_LOAD_SLOTS)

        prompt_tile(load_chunk)

    @pl.when((s > 0) & (s < n_tiles))
    def _():
        prompt_tile()

    @pl.when(s == n_tiles)
    def _():
        op_ref[...] = ln(y_scr[...])
        xs = xs_ref[...]
        ys_scr[...] = DEEPNORM_ALPHA * xs + 0.5 * _swiglu(xs.astype(_BF16), wg_v, wu_v, wd_v)

    @pl.when(s == n_tiles + 1)
    def _():
        os_ref[...] = ln(ys_scr[...])


def _ffn(xp, xs, wg, wu, wd, ln_g, ln_b, *, layer, ln_row):
    m, ms = xp.shape[0], xs.shape[0]
    tm = FFN_TOKEN_TILE
    assert m % tm == 0 and tm % (len(FFN_CHUNKS) * 2 * V7X_SUBLANES) == 0
    n_tiles = m // tm
    hbm = pl.BlockSpec(memory_space=pl.ANY)
    return pl.pallas_call(
        functools.partial(_ffn_kernel, layer=layer, ln_row=ln_row, n_tiles=n_tiles),
        grid=(n_tiles + 2,),
        in_specs=[
            pl.BlockSpec((tm, D_MODEL), lambda i: (jnp.minimum(i, n_tiles - 1), 0)),
            pl.BlockSpec((ms, D_MODEL), lambda i: (0, 0)),
            hbm, hbm, hbm,
            _layer_resident(ln_g.shape, layer), _layer_resident(ln_b.shape, layer),
        ],
        out_specs=[pl.BlockSpec((tm, D_MODEL), lambda i: (jnp.clip(i - 1, 0, n_tiles - 1), 0)),
                   pl.BlockSpec((ms, D_MODEL), lambda i: (0, 0))],
        out_shape=[jax.ShapeDtypeStruct((m, D_MODEL), _F32),
                   jax.ShapeDtypeStruct((ms, D_MODEL), _F32)],
        scratch_shapes=[pltpu.VMEM((D_MODEL, D_FF), _BF16),
                        pltpu.VMEM((D_MODEL, D_FF), _BF16),
                        pltpu.VMEM((D_FF, D_MODEL), _BF16),
                        pltpu.VMEM((FFN_LOAD_SLOTS, 2, D_MODEL, FFN_LOAD_COLS), _F32),
                        pltpu.VMEM((FFN_LOAD_SLOTS, FFN_LOAD_COLS, D_MODEL), _F32),
                        pltpu.SemaphoreType.DMA((FFN_LOAD_SLOTS, 3)),
                        pltpu.VMEM((tm, D_MODEL), _F32),
                        pltpu.VMEM((ms, D_MODEL), _F32)],
        compiler_params=pltpu.CompilerParams(
            dimension_semantics=("arbitrary",),
            vmem_limit_bytes=V7X_VMEM_LIMIT_BYTES),
        name="ffn",
    )(xp, xs, wg, wu, wd, ln_g, ln_b)


def _mixer_kernel(x_ref, h0_ref, cst_ref, pst_ref, win_ref, cw_ref, cb_ref, gw_ref, gab_ref,
                  gxb_ref, lam_ref, pw_ref, ps_ref, wout_ref, g_ref, b_ref,
                  o_ref, ht_ref, cnew_ref, pnew_ref,
                  ul_ext, up_ext, h_scr, ga_scr, gx_scr, xc_scr, hs_scr, gate_scr, x_scr,
                  *, nb, tt, n_chunks, start_pos, layer):
    s = pl.program_id(0)
    pipelined = n_chunks > 1
    m = tt * nb
    hr = HIST * nb
    gw = GROUP_WIDTH
    half = gw // 2
    vec = lambda ref: ref[layer:layer + 1, :]

    @pl.when(s == 0)
    def _():
        h_scr[...] = h0_ref[...]
        ul_ext[0:hr, :] = cst_ref[...]
        if pipelined:
            for ref in (ga_scr, gx_scr, xc_scr, gate_scr, x_scr):
                ref[...] = jnp.zeros(ref.shape, _F32)
            up_ext[0:m, :] = jnp.zeros((m, gw), _F32)
            up_ext[m:m + hr, :] = pst_ref[...]
        else:
            up_ext[0:hr, :] = pst_ref[...]

    def project():
        x = x_ref[...]
        return x, _dot(x.astype(_BF16), win_ref[...])

    def conv_and_gates(proj):
        u_lru = proj[:, 0:gw]
        gate_scr[...] = proj[:, gw:2 * gw]
        ul_ext[hr:hr + m, :] = u_lru
        up_ext[hr:hr + m, :] = proj[:, 2 * gw:3 * gw]
        xc = vec(cb_ref) + cw_ref[CONV_W - 1:CONV_W, :] * u_lru
        for k in range(CONV_W - 1):
            off = hr - (CONV_W - 1 - k) * nb
            xc = xc + cw_ref[k:k + 1, :] * ul_ext[off:off + m, :]
        xc_scr[...] = xc
        xcb = xc.astype(_BF16)
        for p in range(2):
            cols = slice(p * half, (p + 1) * half)
            gts = _dot(xcb[:, cols], gw_ref[p])
            ga_scr[:, cols] = gts[:, 0:half]
            gx_scr[:, cols] = gts[:, half:2 * half]
        ul_ext[0:hr, :] = ul_ext[m:m + hr, :]

    def recurrence_and_mix(chunk, live):
        xc = xc_scr[...]
        r = jax.nn.sigmoid(ga_scr[...] + vec(gab_ref))
        i = jax.nn.sigmoid(gx_scr[...] + vec(gxb_ref))
        nl = -vec(lam_ref)
        softplus_nl = jnp.maximum(nl, 0.0) + jnp.log1p(jnp.exp(-jnp.abs(nl)))
        log_a = -LRU_C * r * softplus_nl
        a = jnp.exp(log_a)
        mult = jnp.sqrt(1.0 - a * a)
        uu = mult * (i * xc)
        h_in = h_scr[...]
        h = h_in
        for t in range(tt):
            rows = slice(t * nb, (t + 1) * nb)
            h = a[rows, :] * h + uu[rows, :]
            hs_scr[rows, :] = h
        if live is not None:
            h = jnp.where(live, h, h_in)
        h_scr[...] = h
        ht_ref[...] = h
        y_lru = hs_scr[...] * jax.nn.gelu(gate_scr[...])

        t_loc = lax.broadcasted_iota(jnp.int32, (tt, nb, V7X_LANES), 0).reshape(m, V7X_LANES)
        pos = start_pos + chunk * tt + t_loc
        pooled = []
        for gi, w in enumerate(POOL_WINDOWS):
            lanes = slice(gi * V7X_LANES, (gi + 1) * V7X_LANES)
            n_rows = (w - 1) * nb + m
            sums = up_ext[hr - (w - 1) * nb:hr + m, lanes]
            span = 1
            while span < w:
                n_rows -= span * nb
                sums = sums[span * nb:span * nb + n_rows, :] + sums[0:n_rows, :]
                span *= 2
            cnt = jnp.minimum(w, pos + 1).astype(_F32)
            pooled.append(sums / cnt - up_ext[hr:hr + m, lanes])
        pooled = jnp.concatenate(pooled, axis=-1).astype(_BF16)
        y_pool = jnp.concatenate(
            [_dot(pooled[:, p * half:(p + 1) * half], pw_ref[p]) for p in range(2)],
            axis=-1) * vec(ps_ref)

        mix = _dot(jnp.concatenate([y_lru, y_pool], axis=-1).astype(_BF16), wout_ref[...])
        up_tail = up_ext[m:m + hr, :]
        up_ext[0:hr, :] = up_tail
        pnew_ref[...] = up_tail
        return mix

    def residual_norm(mix):
        y = DEEPNORM_ALPHA * x_scr[...] + mix
        o_ref[...] = _layernorm(y, g_ref[1:2, :], b_ref[1:2, :])

    x, proj = project()
    if pipelined:
        mix = recurrence_and_mix(jnp.maximum(s - 1, 0), s > 0)
        conv_and_gates(proj)
        residual_norm(mix)
        x_scr[...] = x
    else:
        x_scr[...] = x
        conv_and_gates(proj)
        residual_norm(recurrence_and_mix(s, None))

    @pl.when(s == n_chunks - 1)
    def _():
        cnew_ref[...] = ul_ext[0:hr, :]


def _mixer(x, h0, cst, pst, win, cw, cb, gwt, gab, gxb, lam, pw, ps, wout, ln_g, ln_b,
           *, layer, nb, start_pos):
    n_rows = x.shape[0]
    t_total = n_rows // nb
    tt = min(MIXER_TIME_TILE, t_total)
    assert n_rows % nb == 0 and t_total % tt == 0 and nb % V7X_SUBLANES == 0
    m = tt * nb
    n_chunks = t_total // tt
    n_steps = n_chunks + 1 if n_chunks > 1 else 1
    hr = HIST * nb
    gw = GROUP_WIDTH
    kern = functools.partial(_mixer_kernel, nb=nb, tt=tt, n_chunks=n_chunks, start_pos=start_pos,
                             layer=layer)
    stacked = lambda p: _layer_resident(p.shape, layer)
    out_blk = (lambda i: (jnp.maximum(i - 1, 0), 0)) if n_chunks > 1 else (lambda i: (0, 0))
    return pl.pallas_call(
        kern,
        grid=(n_steps,),
        in_specs=[pl.BlockSpec((m, D_MODEL), lambda i: (jnp.minimum(i, n_chunks - 1), 0)),
                  _resident(h0.shape), _resident(cst.shape), _resident(pst.shape),
                  stacked(win), stacked(cw), _resident(cb.shape), stacked(gwt),
                  _resident(gab.shape), _resident(gxb.shape), _resident(lam.shape),
                  stacked(pw), _resident(ps.shape), stacked(wout), stacked(ln_g), stacked(ln_b)],
        out_specs=[pl.BlockSpec((m, D_MODEL), out_blk),
                   pl.BlockSpec((nb, gw), lambda i: (0, 0)),
                   pl.BlockSpec((hr, gw), lambda i: (0, 0)),
                   pl.BlockSpec((hr, gw), lambda i: (0, 0))],
        out_shape=[jax.ShapeDtypeStruct(x.shape, _F32),
                   jax.ShapeDtypeStruct((nb, gw), _F32),
                   jax.ShapeDtypeStruct((hr, gw), _F32),
                   jax.ShapeDtypeStruct((hr, gw), _F32)],
        scratch_shapes=[pltpu.VMEM((hr + m, gw), _F32),
                        pltpu.VMEM((hr + m, gw), _F32),
                        pltpu.VMEM((nb, gw), _F32),
                        pltpu.VMEM((m, gw), _F32),
                        pltpu.VMEM((m, gw), _F32),
                        pltpu.VMEM((m, gw), _F32),
                        pltpu.VMEM((m, gw), _F32),
                        pltpu.VMEM((m, gw), _F32),
                        pltpu.VMEM((m, D_MODEL), _F32)],
        compiler_params=pltpu.CompilerParams(
            dimension_semantics=("arbitrary",),
            vmem_limit_bytes=V7X_VMEM_LIMIT_BYTES),
        name="mixer",
    )(x, h0, cst, pst, win, cw, cb, gwt, gab, gxb, lam, pw, ps, wout, ln_g, ln_b)


def _block_diag_halves(w):
    depth, groups, n, _ = w.shape
    eye = jnp.eye(groups, dtype=w.dtype)
    full = (w[:, :, :, None, :] * eye[None, :, None, :, None]).reshape(
        depth, groups * n, groups * n)
    h = groups * n // 2
    return jnp.stack([full[:, :h, :h], full[:, h:, h:]], axis=1)


def _state_to_time_major(state):
    depth, nb, k, c = state.shape
    tm = jnp.transpose(state, (0, 2, 1, 3))
    tm = jnp.concatenate([jnp.zeros((depth, HIST - k, nb, c), state.dtype), tm], axis=1)
    return tm.reshape(depth, HIST * nb, c)


def _state_from_time_major(hist, nb, k):
    depth = hist.shape[0]
    tm = hist.reshape(depth, HIST, nb, hist.shape[-1])[:, HIST - k:]
    return jnp.transpose(tm, (0, 2, 1, 3))


def kernel(x_prompt, x_sample, state_lru_h, state_conv, state_pool, ln_g, ln_b,
           w1_gate, w1_up, w1_down, w_in, conv_w, conv_b, gate_a_w, gate_a_b,
           gate_x_w, gate_x_b, lru_lambda, pool_w, pool_scale, w_out,
           w2_gate, w2_up, w2_down):
    batch, seq, d = x_prompt.shape
    dec_batch = x_sample.shape[0]
    gw = GROUP_WIDTH

    win = w_in.astype(_BF16)
    wout = w_out.astype(_BF16)
    gate_w = jnp.concatenate(
        [_block_diag_halves(gate_a_w), _block_diag_halves(gate_x_w)], axis=-1).astype(_BF16)
    pool_wb = _block_diag_halves(pool_w).astype(_BF16)
    mix_params = (win, conv_w, conv_b, gate_w, gate_a_b, gate_x_b, lru_lambda, pool_wb,
                  pool_scale, wout, ln_g, ln_b)

    xp = jnp.transpose(x_prompt, (1, 0, 2)).reshape(seq * batch, d)
    xs = x_sample.reshape(dec_batch, d)
    zeros_h = jnp.zeros((batch, gw), _F32)
    zeros_hist = jnp.zeros((HIST * batch, gw), _F32)
    s_conv_tm = _state_to_time_major(state_conv)
    s_pool_tm = _state_to_time_major(state_pool)

    p_h, p_conv, p_pool, s_h, s_conv, s_pool = [], [], [], [], [], []
    for l in range(DEPTH):
        xp, xs = _ffn(xp, xs, w1_gate, w1_up, w1_down, ln_g, ln_b, layer=l, ln_row=0)
        xp, h_t, cnew, pnew = _mixer(xp, zeros_h, zeros_hist, zeros_hist, *mix_params,
                                     layer=l, nb=batch, start_pos=0)
        p_h.append(h_t); p_conv.append(cnew); p_pool.append(pnew)
        xs, h_t, cnew, pnew = _mixer(xs, state_lru_h[l], s_conv_tm[l], s_pool_tm[l], *mix_params,
                                     layer=l, nb=dec_batch, start_pos=PAST_LEN)
        s_h.append(h_t); s_conv.append(cnew); s_pool.append(pnew)
        xp, xs = _ffn(xp, xs, w2_gate, w2_up, w2_down, ln_g, ln_b, layer=l, ln_row=2)

    y_prompt = jnp.transpose(xp.reshape(seq, batch, d), (1, 0, 2))
    y_sample = xs.reshape(dec_batch, 1, d)
    return (y_prompt, y_sample,
            jnp.stack(p_h),
            _state_from_time_major(jnp.stack(p_conv), batch, CONV_W - 1),
            _state_from_time_major(jnp.stack(p_pool), batch, POOL_MAX_WIN - 1),
            jnp.stack(s_h),
            _state_from_time_major(jnp.stack(s_conv), dec_batch, CONV_W - 1),
            _state_from_time_major(jnp.stack(s_pool), dec_batch, POOL_MAX_WIN - 1))
```

```python
import functools

import jax
import jax.numpy as jnp
from jax import lax
from jax.experimental import pallas as pl
from jax.experimental.pallas import tpu as pltpu

D_MODEL = 1024
DEPTH = 4
GROUP_WIDTH = 512
LRU_C = 8.0
CONV_W = 4
POOL_WINDOWS = (2, 4, 8, 16)
POOL_MAX_WIN = 16
D_FF = 2816
LN_EPS = 1e-5
DEEPNORM_ALPHA = (2.0 * DEPTH) ** 0.25
PAST_LEN = 16384

V7X_LANES = 128
V7X_SUBLANES = 8
V7X_MXU_DIM = 256
V7X_VMEM_LIMIT_BYTES = 56 * 1024 * 1024

HIST = POOL_MAX_WIN
FFN_TOKEN_TILE = 512
MIXER_TIME_TILE = 64
FFN_CHUNKS = (768, 768, 768, 512)
assert sum(FFN_CHUNKS) == D_FF and all(c % V7X_MXU_DIM == 0 for c in FFN_CHUNKS)
FFN_LOAD_COLS = V7X_MXU_DIM
FFN_LOAD_SLOTS = 3
assert D_FF % FFN_LOAD_COLS == 0 and all(c % FFN_LOAD_COLS == 0 for c in FFN_CHUNKS)

_BF16 = jnp.bfloat16
_F32 = jnp.float32


def _dot(a, b):
    return jnp.dot(a, b, preferred_element_type=_F32)


def _layernorm(y, g, b):
    mu = jnp.mean(y, axis=-1, keepdims=True)
    d = y - mu
    var = jnp.mean(d * d, axis=-1, keepdims=True)
    return d * lax.rsqrt(var + LN_EPS) * g + b


def _zero_tile_from(v, shape):
    u = lax.bitcast_convert_type(v, jnp.uint32)
    rows, cols = u.shape
    bf16_rows = 2 * V7X_SUBLANES
    acc = u[0:bf16_rows, :]
    for k in range(1, rows // bf16_rows):
        acc = acc | u[k * bf16_rows:(k + 1) * bf16_rows, :]
    word = acc[:, 0:V7X_LANES]
    for k in range(1, cols // V7X_LANES):
        word = word | acc[:, k * V7X_LANES:(k + 1) * V7X_LANES]
    zero = lax.bitcast_convert_type((word >> 16) >> 16, _F32).astype(_BF16)
    zero = jnp.concatenate([zero] * (shape[1] // V7X_LANES), axis=1)
    return jnp.concatenate([zero] * (shape[0] // bf16_rows), axis=0)


def _resident(shape):
    nd = len(shape)
    return pl.BlockSpec(shape, lambda *_: (0,) * nd, pipeline_mode=pl.Buffered(1))


def _layer_resident(shape, layer):
    nd = len(shape)
    return pl.BlockSpec((None,) + tuple(shape[1:]), lambda *_: (layer,) + (0,) * (nd - 1),
                        pipeline_mode=pl.Buffered(1))


def _ffn_weight_loader(layer, wg_hbm, wu_hbm, wd_hbm, wg_v, wu_v, wd_v, st_gu, st_d, sems):
    n_pieces = D_FF // FFN_LOAD_COLS

    def copies(p):
        slot = p % FFN_LOAD_SLOTS
        cols = pl.ds(p * FFN_LOAD_COLS, FFN_LOAD_COLS)
        return (pltpu.make_async_copy(wg_hbm.at[layer, :, cols], st_gu.at[slot, 0], sems.at[slot, 0]),
                pltpu.make_async_copy(wu_hbm.at[layer, :, cols], st_gu.at[slot, 1], sems.at[slot, 1]),
                pltpu.make_async_copy(wd_hbm.at[layer, cols, :], st_d.at[slot], sems.at[slot, 2]))

    def start(p):
        if p < n_pieces:
            for cp in copies(p):
                cp.start()

    def finish(p):
        slot = p % FFN_LOAD_SLOTS
        for cp in copies(p):
            cp.wait()
        lo = p * FFN_LOAD_COLS
        wg_v[:, lo:lo + FFN_LOAD_COLS] = st_gu[slot, 0].astype(_BF16)
        wu_v[:, lo:lo + FFN_LOAD_COLS] = st_gu[slot, 1].astype(_BF16)
        wd_v[lo:lo + FFN_LOAD_COLS, :] = st_d[slot].astype(_BF16)

    return start, finish


def _swiglu(xb, wg_v, wu_v, wd_v, per_chunk=None, before_chunk=None):
    acc = jnp.zeros((xb.shape[0], D_MODEL), _F32)
    lo = 0
    for ci, width in enumerate(FFN_CHUNKS):
        if before_chunk is not None:
            before_chunk(lo, width)
        gate = _dot(xb, wg_v[:, lo:lo + width])
        up = _dot(xb, wu_v[:, lo:lo + width])
        hid = (jax.nn.silu(gate) * up).astype(_BF16)
        if per_chunk is not None:
            hid = per_chunk(ci, hid)
        acc = acc + _dot(hid, wd_v[lo:lo + width, :])
        lo += width
    return acc


def _ffn_kernel(xp_ref, xs_ref, wg_hbm, wu_hbm, wd_hbm, g_ref, b_ref, op_ref, os_ref,
                wg_v, wu_v, wd_v, st_gu, st_d, sems, y_scr, ys_scr,
                *, layer, ln_row, n_tiles, layout):
    s = pl.program_id(0)
    ln = lambda y: _layernorm(y, g_ref[ln_row:ln_row + 1, :], b_ref[ln_row:ln_row + 1, :])

    if layout == "rows":
        load_tile = lambda: xp_ref[...]
        def store_rows(r0, r1, val):
            op_ref[r0:r1, :] = val
    elif layout == "batch_to_time":
        nbat, tt = xp_ref.shape[0], xp_ref.shape[1]
        load_tile = lambda: xp_ref[...].reshape(nbat * tt, D_MODEL)
        def store_rows(r0, r1, val):
            for b in range(r0 // tt, r1 // tt):
                op_ref[:, b, :] = val[b * tt - r0:(b + 1) * tt - r0, :]
    else:
        tt, nbat = xp_ref.shape[0], xp_ref.shape[1]
        load_tile = lambda: jnp.concatenate([xp_ref[:, b, :] for b in range(nbat)], axis=0)
        def store_rows(r0, r1, val):
            op_ref[r0 // tt:r1 // tt, :, :] = val.reshape((r1 - r0) // tt, tt, D_MODEL)

    def prompt_tile(before_chunk=None):
        x = load_tile()
        part = x.shape[0] // len(FFN_CHUNKS)

        def ln_part(ci, hid):
            rows = slice(ci * part, (ci + 1) * part)
            normed = ln(y_scr[rows, :])
            store_rows(ci * part, (ci + 1) * part, normed)
            return hid + _zero_tile_from(normed, hid.shape)

        acc = _swiglu(x.astype(_BF16), wg_v, wu_v, wd_v, ln_part, before_chunk)
        y_scr[...] = DEEPNORM_ALPHA * x + 0.5 * acc

    @pl.when(s == 0)
    def _():
        y_scr[...] = jnp.zeros(y_scr.shape, _F32)
        start, finish = _ffn_weight_loader(layer, wg_hbm, wu_hbm, wd_hbm, wg_v, wu_v, wd_v,
                                           st_gu, st_d, sems)
        for p in range(FFN_LOAD_SLOTS):
            start(p)

        def load_chunk(lo, width):
            for p in range(lo // FFN_LOAD_COLS, (lo + width) // FFN_LOAD_COLS):
                finish(p)
                start(p + FFN_LOAD_SLOTS)

        prompt_tile(load_chunk)

    @pl.when((s > 0) & (s < n_tiles))
    def _():
        prompt_tile()

    @pl.when(s == n_tiles)
    def _():
        store_rows(0, y_scr.shape[0], ln(y_scr[...]))
        xs = xs_ref[...]
        ys_scr[...] = DEEPNORM_ALPHA * xs + 0.5 * _swiglu(xs.astype(_BF16), wg_v, wu_v, wd_v)

    @pl.when(s == n_tiles + 1)
    def _():
        os_ref[...] = ln(ys_scr[...])


def _ffn(xp, xs, wg, wu, wd, ln_g, ln_b, *, layer, ln_row, layout="rows"):
    ms = xs.shape[0]
    tm = FFN_TOKEN_TILE
    assert tm % (len(FFN_CHUNKS) * 2 * V7X_SUBLANES) == 0
    if layout == "rows":
        m = xp.shape[0]
        in_block, out_block, out_full = (tm, D_MODEL), (tm, D_MODEL), (m, D_MODEL)
        at = lambda j: (j, 0)
        in_at = out_at = at
    else:
        nbat, t_total = (xp.shape[0], xp.shape[1]) if layout == "batch_to_time" else (xp.shape[1], xp.shape[0])
        m = nbat * t_total
        tt = tm // nbat
        part_rows = tm // len(FFN_CHUNKS)
        assert tm % nbat == 0 and tt % V7X_SUBLANES == 0 and part_rows % tt == 0
        batch_major = ((nbat, tt, D_MODEL), lambda j: (0, j, 0), (nbat, t_total, D_MODEL))
        time_major = ((tt, nbat, D_MODEL), lambda j: (j, 0, 0), (t_total, nbat, D_MODEL))
        src, dst = (batch_major, time_major) if layout == "batch_to_time" else (time_major, batch_major)
        in_block, in_at = src[0], src[1]
        out_block, out_at, out_full = dst
    assert m % tm == 0
    n_tiles = m // tm
    hbm = pl.BlockSpec(memory_space=pl.ANY)
    return pl.pallas_call(
        functools.partial(_ffn_kernel, layer=layer, ln_row=ln_row, n_tiles=n_tiles, layout=layout),
        grid=(n_tiles + 2,),
        in_specs=[
            pl.BlockSpec(in_block, lambda i: in_at(jnp.minimum(i, n_tiles - 1))),
            pl.BlockSpec((ms, D_MODEL), lambda i: (0, 0)),
            hbm, hbm, hbm,
            _layer_resident(ln_g.shape, layer), _layer_resident(ln_b.shape, layer),
        ],
        out_specs=[pl.BlockSpec(out_block, lambda i: out_at(jnp.clip(i - 1, 0, n_tiles - 1))),
                   pl.BlockSpec((ms, D_MODEL), lambda i: (0, 0))],
        out_shape=[jax.ShapeDtypeStruct(out_full, _F32),
                   jax.ShapeDtypeStruct((ms, D_MODEL), _F32)],
        scratch_shapes=[pltpu.VMEM((D_MODEL, D_FF), _BF16),
                        pltpu.VMEM((D_MODEL, D_FF), _BF16),
                        pltpu.VMEM((D_FF, D_MODEL), _BF16),
                        pltpu.VMEM((FFN_LOAD_SLOTS, 2, D_MODEL, FFN_LOAD_COLS), _F32),
                        pltpu.VMEM((FFN_LOAD_SLOTS, FFN_LOAD_COLS, D_MODEL), _F32),
                        pltpu.SemaphoreType.DMA((FFN_LOAD_SLOTS, 3)),
                        pltpu.VMEM((tm, D_MODEL), _F32),
                        pltpu.VMEM((ms, D_MODEL), _F32)],
        compiler_params=pltpu.CompilerParams(
            dimension_semantics=("arbitrary",),
            vmem_limit_bytes=V7X_VMEM_LIMIT_BYTES),
        name="ffn",
    )(xp, xs, wg, wu, wd, ln_g, ln_b)


def _mixer_kernel(x_ref, h0_ref, cst_ref, pst_ref, win_ref, cw_ref, cb_ref, gw_ref, gab_ref,
                  gxb_ref, lam_ref, pw_ref, ps_ref, wout_ref, g_ref, b_ref,
                  o_ref, ht_ref, cnew_ref, pnew_ref,
                  ul_ext, up_ext, h_scr, ga_scr, gx_scr, xc_scr, hs_scr, gate_scr, x_scr,
                  *, nb, tt, n_chunks, start_pos, layer):
    s = pl.program_id(0)
    pipelined = n_chunks > 1
    m = tt * nb
    hr = HIST * nb
    gw = GROUP_WIDTH
    half = gw // 2
    vec = lambda ref: ref[layer:layer + 1, :]

    @pl.when(s == 0)
    def _():
        h_scr[...] = h0_ref[...]
        ul_ext[0:hr, :] = cst_ref[...]
        if pipelined:
            for ref in (ga_scr, gx_scr, xc_scr, gate_scr, x_scr):
                ref[...] = jnp.zeros(ref.shape, _F32)
            up_ext[0:m, :] = jnp.zeros((m, gw), _F32)
            up_ext[m:m + hr, :] = pst_ref[...]
        else:
            up_ext[0:hr, :] = pst_ref[...]

    def project():
        x = x_ref[...]
        return x, _dot(x.astype(_BF16), win_ref[...])

    def conv_and_gates(proj):
        u_lru = proj[:, 0:gw]
        gate_scr[...] = proj[:, gw:2 * gw]
        ul_ext[hr:hr + m, :] = u_lru
        up_ext[hr:hr + m, :] = proj[:, 2 * gw:3 * gw]
        xc = vec(cb_ref) + cw_ref[CONV_W - 1:CONV_W, :] * u_lru
        for k in range(CONV_W - 1):
            off = hr - (CONV_W - 1 - k) * nb
            xc = xc + cw_ref[k:k + 1, :] * ul_ext[off:off + m, :]
        xc_scr[...] = xc
        xcb = xc.astype(_BF16)
        for p in range(2):
            cols = slice(p * half, (p + 1) * half)
            gts = _dot(xcb[:, cols], gw_ref[p])
            ga_scr[:, cols] = gts[:, 0:half]
            gx_scr[:, cols] = gts[:, half:2 * half]
        ul_ext[0:hr, :] = ul_ext[m:m + hr, :]

    def recurrence_and_mix(chunk, live):
        xc = xc_scr[...]
        r = jax.nn.sigmoid(ga_scr[...] + vec(gab_ref))
        i = jax.nn.sigmoid(gx_scr[...] + vec(gxb_ref))
        nl = -vec(lam_ref)
        softplus_nl = jnp.maximum(nl, 0.0) + jnp.log1p(jnp.exp(-jnp.abs(nl)))
        log_a = -LRU_C * r * softplus_nl
        a = jnp.exp(log_a)
        mult = jnp.sqrt(1.0 - a * a)
        uu = mult * (i * xc)
        h_in = h_scr[...]
        h = h_in
        for t in range(tt):
            rows = slice(t * nb, (t + 1) * nb)
            h = a[rows, :] * h + uu[rows, :]
            hs_scr[rows, :] = h
        if live is not None:
            h = jnp.where(live, h, h_in)
        h_scr[...] = h
        ht_ref[...] = h
        y_lru = hs_scr[...] * jax.nn.gelu(gate_scr[...])

        t_loc = lax.broadcasted_iota(jnp.int32, (tt, nb, V7X_LANES), 0).reshape(m, V7X_LANES)
        pos = start_pos + chunk * tt + t_loc
        pooled = []
        for gi, w in enumerate(POOL_WINDOWS):
            lanes = slice(gi * V7X_LANES, (gi + 1) * V7X_LANES)
            n_rows = (w - 1) * nb + m
            sums = up_ext[hr - (w - 1) * nb:hr + m, lanes]
            span = 1
            while span < w:
                n_rows -= span * nb
                sums = sums[span * nb:span * nb + n_rows, :] + sums[0:n_rows, :]
                span *= 2
            cnt = jnp.minimum(w, pos + 1).astype(_F32)
            pooled.append(sums / cnt - up_ext[hr:hr + m, lanes])
        pooled = jnp.concatenate(pooled, axis=-1).astype(_BF16)
        y_pool = jnp.concatenate(
            [_dot(pooled[:, p * half:(p + 1) * half], pw_ref[p]) for p in range(2)],
            axis=-1) * vec(ps_ref)

        mix = _dot(jnp.concatenate([y_lru, y_pool], axis=-1).astype(_BF16), wout_ref[...])
        up_tail = up_ext[m:m + hr, :]
        up_ext[0:hr, :] = up_tail
        pnew_ref[...] = up_tail
        return mix

    def residual_norm(mix):
        y = DEEPNORM_ALPHA * x_scr[...] + mix
        o_ref[...] = _layernorm(y, g_ref[1:2, :], b_ref[1:2, :])

    x, proj = project()
    if pipelined:
        mix = recurrence_and_mix(jnp.maximum(s - 1, 0), s > 0)
        conv_and_gates(proj)
        residual_norm(mix)
        x_scr[...] = x
    else:
        x_scr[...] = x
        conv_and_gates(proj)
        residual_norm(recurrence_and_mix(s, None))

    @pl.when(s == n_chunks - 1)
    def _():
        cnew_ref[...] = ul_ext[0:hr, :]


def _mixer(x, h0, cst, pst, win, cw, cb, gwt, gab, gxb, lam, pw, ps, wout, ln_g, ln_b,
           *, layer, nb, start_pos):
    n_rows = x.shape[0]
    t_total = n_rows // nb
    tt = min(MIXER_TIME_TILE, t_total)
    assert n_rows % nb == 0 and t_total % tt == 0 and nb % V7X_SUBLANES == 0
    m = tt * nb
    n_chunks = t_total // tt
    n_steps = n_chunks + 1 if n_chunks > 1 else 1
    hr = HIST * nb
    gw = GROUP_WIDTH
    kern = functools.partial(_mixer_kernel, nb=nb, tt=tt, n_chunks=n_chunks, start_pos=start_pos,
                             layer=layer)
    stacked = lambda p: _layer_resident(p.shape, layer)
    out_blk = (lambda i: (jnp.maximum(i - 1, 0), 0)) if n_chunks > 1 else (lambda i: (0, 0))
    return pl.pallas_call(
        kern,
        grid=(n_steps,),
        in_specs=[pl.BlockSpec((m, D_MODEL), lambda i: (jnp.minimum(i, n_chunks - 1), 0)),
                  _resident(h0.shape), _resident(cst.shape), _resident(pst.shape),
                  stacked(win), stacked(cw), _resident(cb.shape), stacked(gwt),
                  _resident(gab.shape), _resident(gxb.shape), _resident(lam.shape),
                  stacked(pw), _resident(ps.shape), stacked(wout), stacked(ln_g), stacked(ln_b)],
        out_specs=[pl.BlockSpec((m, D_MODEL), out_blk),
                   pl.BlockSpec((nb, gw), lambda i: (0, 0)),
                   pl.BlockSpec((hr, gw), lambda i: (0, 0)),
                   pl.BlockSpec((hr, gw), lambda i: (0, 0))],
        out_shape=[jax.ShapeDtypeStruct(x.shape, _F32),
                   jax.ShapeDtypeStruct((nb, gw), _F32),
                   jax.ShapeDtypeStruct((hr, gw), _F32),
                   jax.ShapeDtypeStruct((hr, gw), _F32)],
        scratch_shapes=[pltpu.VMEM((hr + m, gw), _F32),
                        pltpu.VMEM((hr + m, gw), _F32),
                        pltpu.VMEM((nb, gw), _F32),
                        pltpu.VMEM((m, gw), _F32),
                        pltpu.VMEM((m, gw), _F32),
                        pltpu.VMEM((m, gw), _F32),
                        pltpu.VMEM((m, gw), _F32),
                        pltpu.VMEM((m, gw), _F32),
                        pltpu.VMEM((m, D_MODEL), _F32)],
        compiler_params=pltpu.CompilerParams(
            dimension_semantics=("arbitrary",),
            vmem_limit_bytes=V7X_VMEM_LIMIT_BYTES),
        name="mixer",
    )(x, h0, cst, pst, win, cw, cb, gwt, gab, gxb, lam, pw, ps, wout, ln_g, ln_b)


def _block_diag_halves(w):
    depth, groups, n, _ = w.shape
    eye = jnp.eye(groups, dtype=w.dtype)
    full = (w[:, :, :, None, :] * eye[None, :, None, :, None]).reshape(
        depth, groups * n, groups * n)
    h = groups * n // 2
    return jnp.stack([full[:, :h, :h], full[:, h:, h:]], axis=1)


def _state_to_time_major(state):
    depth, nb, k, c = state.shape
    tm = jnp.transpose(state, (0, 2, 1, 3))
    tm = jnp.concatenate([jnp.zeros((depth, HIST - k, nb, c), state.dtype), tm], axis=1)
    return tm.reshape(depth, HIST * nb, c)


def _state_from_time_major(hist, nb, k):
    depth = hist.shape[0]
    tm = hist.reshape(depth, HIST, nb, hist.shape[-1])[:, HIST - k:]
    return jnp.transpose(tm, (0, 2, 1, 3))


def kernel(x_prompt, x_sample, state_lru_h, state_conv, state_pool, ln_g, ln_b,
           w1_gate, w1_up, w1_down, w_in, conv_w, conv_b, gate_a_w, gate_a_b,
           gate_x_w, gate_x_b, lru_lambda, pool_w, pool_scale, w_out,
           w2_gate, w2_up, w2_down):
    batch, seq, d = x_prompt.shape
    dec_batch = x_sample.shape[0]
    gw = GROUP_WIDTH

    win = w_in.astype(_BF16)
    wout = w_out.astype(_BF16)
    gate_w = jnp.concatenate(
        [_block_diag_halves(gate_a_w), _block_diag_halves(gate_x_w)], axis=-1).astype(_BF16)
    pool_wb = _block_diag_halves(pool_w).astype(_BF16)
    mix_params = (win, conv_w, conv_b, gate_w, gate_a_b, gate_x_b, lru_lambda, pool_wb,
                  pool_scale, wout, ln_g, ln_b)

    xp = x_prompt
    xs = x_sample.reshape(dec_batch, d)
    zeros_h = jnp.zeros((batch, gw), _F32)
    zeros_hist = jnp.zeros((HIST * batch, gw), _F32)
    s_conv_tm = _state_to_time_major(state_conv)
    s_pool_tm = _state_to_time_major(state_pool)

    p_h, p_conv, p_pool, s_h, s_conv, s_pool = [], [], [], [], [], []
    for l in range(DEPTH):
        xp, xs = _ffn(xp, xs, w1_gate, w1_up, w1_down, ln_g, ln_b, layer=l, ln_row=0,
                      layout="batch_to_time" if l == 0 else "rows")
        xp = xp.reshape(seq * batch, d)
        xp, h_t, cnew, pnew = _mixer(xp, zeros_h, zeros_hist, zeros_hist, *mix_params,
                                     layer=l, nb=batch, start_pos=0)
        p_h.append(h_t); p_conv.append(cnew); p_pool.append(pnew)
        xs, h_t, cnew, pnew = _mixer(xs, state_lru_h[l], s_conv_tm[l], s_pool_tm[l], *mix_params,
                                     layer=l, nb=dec_batch, start_pos=PAST_LEN)
        s_h.append(h_t); s_conv.append(cnew); s_pool.append(pnew)
        if l == DEPTH - 1:
            xp = xp.reshape(seq, batch, d)
        xp, xs = _ffn(xp, xs, w2_gate, w2_up, w2_down, ln_g, ln_b, layer=l, ln_row=2,
                      layout="time_to_batch" if l == DEPTH - 1 else "rows")

    y_sample = xs.reshape(dec_batch, 1, d)
    return (xp, y_sample,
            jnp.stack(p_h),
            _state_from_time_major(jnp.stack(p_conv), batch, CONV_W - 1),
            _state_from_time_major(jnp.stack(p_pool), batch, POOL_MAX_WIN - 1),
            jnp.stack(s_h),
            _state_from_time_major(jnp.stack(s_conv), dec_batch, CONV_W - 1),
            _state_from_time_major(jnp.stack(s_pool), dec_batch, POOL_MAX_WIN - 1))
```

```python
import functools

import jax
import jax.numpy as jnp
from jax import lax
from jax.experimental import pallas as pl
from jax.experimental.pallas import tpu as pltpu

D_MODEL = 1024
DEPTH = 4
GROUP_WIDTH = 512
LRU_C = 8.0
CONV_W = 4
POOL_WINDOWS = (2, 4, 8, 16)
POOL_MAX_WIN = 16
D_FF = 2816
LN_EPS = 1e-5
DEEPNORM_ALPHA = (2.0 * DEPTH) ** 0.25
PAST_LEN = 16384

V7X_LANES = 128
V7X_SUBLANES = 8
V7X_MXU_DIM = 256
V7X_VMEM_LIMIT_BYTES = 56 * 1024 * 1024

HIST = POOL_MAX_WIN
FFN_TOKEN_TILE = 512
MIXER_TIME_TILE = 64
FFN_CHUNKS = (768, 768, 768, 512)
assert sum(FFN_CHUNKS) == D_FF and all(c % V7X_MXU_DIM == 0 for c in FFN_CHUNKS)
FFN_LOAD_COLS = V7X_MXU_DIM
FFN_LOAD_SLOTS = 3
assert D_FF % FFN_LOAD_COLS == 0 and all(c % FFN_LOAD_COLS == 0 for c in FFN_CHUNKS)

_BF16 = jnp.bfloat16
_F32 = jnp.float32


def _dot(a, b):
    return jnp.dot(a, b, preferred_element_type=_F32)


def _layernorm(y, g, b):
    mu = jnp.mean(y, axis=-1, keepdims=True)
    d = y - mu
    var = jnp.mean(d * d, axis=-1, keepdims=True)
    return d * lax.rsqrt(var + LN_EPS) * g + b


def _zero_tile_from(v, shape):
    u = lax.bitcast_convert_type(v, jnp.uint32)
    rows, cols = u.shape
    bf16_rows = 2 * V7X_SUBLANES
    acc = u[0:bf16_rows, :]
    for k in range(1, rows // bf16_rows):
        acc = acc | u[k * bf16_rows:(k + 1) * bf16_rows, :]
    word = acc[:, 0:V7X_LANES]
    for k in range(1, cols // V7X_LANES):
        word = word | acc[:, k * V7X_LANES:(k + 1) * V7X_LANES]
    zero = lax.bitcast_convert_type((word >> 16) >> 16, _F32).astype(_BF16)
    zero = jnp.concatenate([zero] * (shape[1] // V7X_LANES), axis=1)
    return jnp.concatenate([zero] * (shape[0] // bf16_rows), axis=0)


def _resident(shape):
    nd = len(shape)
    return pl.BlockSpec(shape, lambda *_: (0,) * nd, pipeline_mode=pl.Buffered(1))


def _layer_resident(shape, layer):
    nd = len(shape)
    return pl.BlockSpec((None,) + tuple(shape[1:]), lambda *_: (layer,) + (0,) * (nd - 1),
                        pipeline_mode=pl.Buffered(1))


def _ffn_weight_loader(layer, wg_hbm, wu_hbm, wd_hbm, wg_v, wu_v, wd_v, st_gu, st_d, sems):
    n_pieces = D_FF // FFN_LOAD_COLS

    def copies(p):
        slot = p % FFN_LOAD_SLOTS
        cols = pl.ds(p * FFN_LOAD_COLS, FFN_LOAD_COLS)
        return (pltpu.make_async_copy(wg_hbm.at[layer, :, cols], st_gu.at[slot, 0], sems.at[slot, 0]),
                pltpu.make_async_copy(wu_hbm.at[layer, :, cols], st_gu.at[slot, 1], sems.at[slot, 1]),
                pltpu.make_async_copy(wd_hbm.at[layer, cols, :], st_d.at[slot], sems.at[slot, 2]))

    def start(p):
        if p < n_pieces:
            for cp in copies(p):
                cp.start()

    def finish(p):
        slot = p % FFN_LOAD_SLOTS
        for cp in copies(p):
            cp.wait()
        lo = p * FFN_LOAD_COLS
        wg_v[:, lo:lo + FFN_LOAD_COLS] = st_gu[slot, 0].astype(_BF16)
        wu_v[:, lo:lo + FFN_LOAD_COLS] = st_gu[slot, 1].astype(_BF16)
        wd_v[lo:lo + FFN_LOAD_COLS, :] = st_d[slot].astype(_BF16)

    return start, finish


def _swiglu(xb, wg_v, wu_v, wd_v, per_chunk=None, before_chunk=None):
    acc = jnp.zeros((xb.shape[0], D_MODEL), _F32)
    lo = 0
    for ci, width in enumerate(FFN_CHUNKS):
        if before_chunk is not None:
            before_chunk(lo, width)
        gate = _dot(xb, wg_v[:, lo:lo + width])
        up = _dot(xb, wu_v[:, lo:lo + width])
        hid = (jax.nn.silu(gate) * up).astype(_BF16)
        if per_chunk is not None:
            hid = per_chunk(ci, hid)
        acc = acc + _dot(hid, wd_v[lo:lo + width, :])
        lo += width
    return acc


def _ffn_kernel(xp_ref, xs_ref, wg_hbm, wu_hbm, wd_hbm, g_ref, b_ref, op_ref, os_ref,
                wg_v, wu_v, wd_v, st_gu, st_d, sems, y_scr, ys_scr,
                *, layer, ln_row, n_tiles, layout):
    s = pl.program_id(0)
    ln = lambda y: _layernorm(y, g_ref[ln_row:ln_row + 1, :], b_ref[ln_row:ln_row + 1, :])

    if layout == "rows":
        load_tile = lambda: xp_ref[...]
        def store_rows(r0, r1, val):
            op_ref[r0:r1, :] = val
    elif layout == "batch_to_time":
        nbat, tt = xp_ref.shape[0], xp_ref.shape[1]
        load_tile = lambda: xp_ref[...].reshape(nbat * tt, D_MODEL)
        def store_rows(r0, r1, val):
            for b in range(r0 // tt, r1 // tt):
                op_ref[:, b, :] = val[b * tt - r0:(b + 1) * tt - r0, :]
    else:
        tt, nbat = xp_ref.shape[0], xp_ref.shape[1]
        load_tile = lambda: jnp.concatenate([xp_ref[:, b, :] for b in range(nbat)], axis=0)
        def store_rows(r0, r1, val):
            op_ref[r0 // tt:r1 // tt, :, :] = val.reshape((r1 - r0) // tt, tt, D_MODEL)

    def prompt_tile(before_chunk=None):
        x = load_tile()
        part = x.shape[0] // len(FFN_CHUNKS)

        def ln_part(ci, hid):
            rows = slice(ci * part, (ci + 1) * part)
            normed = ln(y_scr[rows, :])
            store_rows(ci * part, (ci + 1) * part, normed)
            return hid + _zero_tile_from(normed, hid.shape)

        acc = _swiglu(x.astype(_BF16), wg_v, wu_v, wd_v, ln_part, before_chunk)
        y_scr[...] = DEEPNORM_ALPHA * x + 0.5 * acc

    @pl.when(s == 0)
    def _():
        y_scr[...] = jnp.zeros(y_scr.shape, _F32)
        start, finish = _ffn_weight_loader(layer, wg_hbm, wu_hbm, wd_hbm, wg_v, wu_v, wd_v,
                                           st_gu, st_d, sems)
        for p in range(FFN_LOAD_SLOTS):
            start(p)

        def load_chunk(lo, width):
            for p in range(lo // FFN_LOAD_COLS, (lo + width) // FFN_LOAD_COLS):
                finish(p)
                start(p + FFN_LOAD_SLOTS)

        prompt_tile(load_chunk)

    @pl.when((s > 0) & (s < n_tiles))
    def _():
        prompt_tile()

    @pl.when(s == n_tiles)
    def _():
        store_rows(0, y_scr.shape[0], ln(y_scr[...]))
        xs = xs_ref[...]
        ys_scr[...] = DEEPNORM_ALPHA * xs + 0.5 * _swiglu(xs.astype(_BF16), wg_v, wu_v, wd_v)

    @pl.when(s == n_tiles + 1)
    def _():
        os_ref[...] = ln(ys_scr[...])


def _ffn(xp, xs, wg, wu, wd, ln_g, ln_b, *, layer, ln_row, layout="rows"):
    ms = xs.shape[0]
    tm = FFN_TOKEN_TILE
    assert tm % (len(FFN_CHUNKS) * 2 * V7X_SUBLANES) == 0
    if layout == "rows":
        m = xp.shape[0]
        in_block, out_block, out_full = (tm, D_MODEL), (tm, D_MODEL), (m, D_MODEL)
        at = lambda j: (j, 0)
        in_at = out_at = at
    else:
        nbat, t_total = (xp.shape[0], xp.shape[1]) if layout == "batch_to_time" else (xp.shape[1], xp.shape[0])
        m = nbat * t_total
        tt = tm // nbat
        part_rows = tm // len(FFN_CHUNKS)
        assert tm % nbat == 0 and tt % V7X_SUBLANES == 0 and part_rows % tt == 0
        batch_major = ((nbat, tt, D_MODEL), lambda j: (0, j, 0), (nbat, t_total, D_MODEL))
        time_major = ((tt, nbat, D_MODEL), lambda j: (j, 0, 0), (t_total, nbat, D_MODEL))
        src, dst = (batch_major, time_major) if layout == "batch_to_time" else (time_major, batch_major)
        in_block, in_at = src[0], src[1]
        out_block, out_at, out_full = dst
    assert m % tm == 0
    n_tiles = m // tm
    hbm = pl.BlockSpec(memory_space=pl.ANY)
    return pl.pallas_call(
        functools.partial(_ffn_kernel, layer=layer, ln_row=ln_row, n_tiles=n_tiles, layout=layout),
        grid=(n_tiles + 2,),
        in_specs=[
            pl.BlockSpec(in_block, lambda i: in_at(jnp.minimum(i, n_tiles - 1))),
            pl.BlockSpec((ms, D_MODEL), lambda i: (0, 0)),
            hbm, hbm, hbm,
            _layer_resident(ln_g.shape, layer), _layer_resident(ln_b.shape, layer),
        ],
        out_specs=[pl.BlockSpec(out_block, lambda i: out_at(jnp.clip(i - 1, 0, n_tiles - 1))),
                   pl.BlockSpec((ms, D_MODEL), lambda i: (0, 0))],
        out_shape=[jax.ShapeDtypeStruct(out_full, _F32),
                   jax.ShapeDtypeStruct((ms, D_MODEL), _F32)],
        scratch_shapes=[pltpu.VMEM((D_MODEL, D_FF), _BF16),
                        pltpu.VMEM((D_MODEL, D_FF), _BF16),
                        pltpu.VMEM((D_FF, D_MODEL), _BF16),
                        pltpu.VMEM((FFN_LOAD_SLOTS, 2, D_MODEL, FFN_LOAD_COLS), _F32),
                        pltpu.VMEM((FFN_LOAD_SLOTS, FFN_LOAD_COLS, D_MODEL), _F32),
                        pltpu.SemaphoreType.DMA((FFN_LOAD_SLOTS, 3)),
                        pltpu.VMEM((tm, D_MODEL), _F32),
                        pltpu.VMEM((ms, D_MODEL), _F32)],
        compiler_params=pltpu.CompilerParams(
            dimension_semantics=("arbitrary",),
            vmem_limit_bytes=V7X_VMEM_LIMIT_BYTES),
        name="ffn",
    )(xp, xs, wg, wu, wd, ln_g, ln_b)


def _mixer_kernel(x_ref, h0_ref, cst_ref, pst_ref, win_ref, cw_ref, cb_ref, gw_ref, gab_ref,
                  gxb_ref, lam_ref, pw_ref, ps_ref, wout_ref, g_ref, b_ref,
                  o_ref, ht_ref, cnew_ref, pnew_ref,
                  ul_ext, up_ext, h_scr, ga_scr, gx_scr, xc_scr, hs_scr, gate_scr, x_scr,
                  *, nb, tt, n_chunks, start_pos, layer):
    s = pl.program_id(0)
    pipelined = n_chunks > 1
    m = tt * nb
    hr = HIST * nb
    gw = GROUP_WIDTH
    half = gw // 2
    vec = lambda ref: ref[layer:layer + 1, :]

    @pl.when(s == 0)
    def _():
        h_scr[...] = h0_ref[...]
        ul_ext[0:hr, :] = jnp.zeros((hr, gw), _F32)
        for k in range(CONV_W - 1):
            j = HIST - (CONV_W - 1) + k
            ul_ext[j * nb:(j + 1) * nb, :] = cst_ref[:, k, :]
        if pipelined:
            for ref in (ga_scr, gx_scr, xc_scr, gate_scr, x_scr):
                ref[...] = jnp.zeros(ref.shape, _F32)
            up_ext[0:m, :] = jnp.zeros((m, gw), _F32)
        pool0 = m if pipelined else 0
        up_ext[pool0:pool0 + nb, :] = jnp.zeros((nb, gw), _F32)
        for k in range(POOL_MAX_WIN - 1):
            j = HIST - (POOL_MAX_WIN - 1) + k
            up_ext[pool0 + j * nb:pool0 + (j + 1) * nb, :] = pst_ref[:, k, :]

    def project():
        x = x_ref[...]
        return x, _dot(x.astype(_BF16), win_ref[...])

    def conv_and_gates(proj):
        u_lru = proj[:, 0:gw]
        gate_scr[...] = proj[:, gw:2 * gw]
        ul_ext[hr:hr + m, :] = u_lru
        up_ext[hr:hr + m, :] = proj[:, 2 * gw:3 * gw]
        xc = vec(cb_ref) + cw_ref[CONV_W - 1:CONV_W, :] * u_lru
        for k in range(CONV_W - 1):
            off = hr - (CONV_W - 1 - k) * nb
            xc = xc + cw_ref[k:k + 1, :] * ul_ext[off:off + m, :]
        xc_scr[...] = xc
        xcb = xc.astype(_BF16)
        for p in range(2):
            cols = slice(p * half, (p + 1) * half)
            gts = _dot(xcb[:, cols], gw_ref[p])
            ga_scr[:, cols] = gts[:, 0:half]
            gx_scr[:, cols] = gts[:, half:2 * half]
        ul_ext[0:hr, :] = ul_ext[m:m + hr, :]

    def recurrence_and_mix(chunk, live):
        xc = xc_scr[...]
        r = jax.nn.sigmoid(ga_scr[...] + vec(gab_ref))
        i = jax.nn.sigmoid(gx_scr[...] + vec(gxb_ref))
        nl = -vec(lam_ref)
        softplus_nl = jnp.maximum(nl, 0.0) + jnp.log1p(jnp.exp(-jnp.abs(nl)))
        log_a = -LRU_C * r * softplus_nl
        a = jnp.exp(log_a)
        mult = jnp.sqrt(1.0 - a * a)
        uu = mult * (i * xc)
        h_in = h_scr[...]
        h = h_in
        for t in range(tt):
            rows = slice(t * nb, (t + 1) * nb)
            h = a[rows, :] * h + uu[rows, :]
            hs_scr[rows, :] = h
        if live is not None:
            h = jnp.where(live, h, h_in)
        h_scr[...] = h
        ht_ref[...] = h
        y_lru = hs_scr[...] * jax.nn.gelu(gate_scr[...])

        t_loc = lax.broadcasted_iota(jnp.int32, (tt, nb, V7X_LANES), 0).reshape(m, V7X_LANES)
        pos = start_pos + chunk * tt + t_loc
        pooled = []
        for gi, w in enumerate(POOL_WINDOWS):
            lanes = slice(gi * V7X_LANES, (gi + 1) * V7X_LANES)
            n_rows = (w - 1) * nb + m
            sums = up_ext[hr - (w - 1) * nb:hr + m, lanes]
            span = 1
            while span < w:
                n_rows -= span * nb
                sums = sums[span * nb:span * nb + n_rows, :] + sums[0:n_rows, :]
                span *= 2
            cnt = jnp.minimum(w, pos + 1).astype(_F32)
            pooled.append(sums / cnt - up_ext[hr:hr + m, lanes])
        pooled = jnp.concatenate(pooled, axis=-1).astype(_BF16)
        y_pool = jnp.concatenate(
            [_dot(pooled[:, p * half:(p + 1) * half], pw_ref[p]) for p in range(2)],
            axis=-1) * vec(ps_ref)

        mix = _dot(jnp.concatenate([y_lru, y_pool], axis=-1).astype(_BF16), wout_ref[...])
        up_ext[0:hr, :] = up_ext[m:m + hr, :]
        return mix

    def residual_norm(mix):
        y = DEEPNORM_ALPHA * x_scr[...] + mix
        o_ref[...] = _layernorm(y, g_ref[1:2, :], b_ref[1:2, :])

    x, proj = project()
    if pipelined:
        mix = recurrence_and_mix(jnp.maximum(s - 1, 0), s > 0)
        conv_and_gates(proj)
        residual_norm(mix)
        x_scr[...] = x
    else:
        x_scr[...] = x
        conv_and_gates(proj)
        residual_norm(recurrence_and_mix(s, None))

    @pl.when(s == n_chunks - 1)
    def _():
        for k in range(CONV_W - 1):
            j = HIST - (CONV_W - 1) + k
            cnew_ref[:, k, :] = ul_ext[j * nb:(j + 1) * nb, :]

    @pl.when(s == pl.num_programs(0) - 1)
    def _():
        for k in range(POOL_MAX_WIN - 1):
            j = HIST - (POOL_MAX_WIN - 1) + k
            pnew_ref[:, k, :] = up_ext[j * nb:(j + 1) * nb, :]


def _mixer(x, h0, cst, pst, win, cw, cb, gwt, gab, gxb, lam, pw, ps, wout, ln_g, ln_b,
           *, layer, state_layer, nb, start_pos):
    n_rows = x.shape[0]
    t_total = n_rows // nb
    tt = min(MIXER_TIME_TILE, t_total)
    assert n_rows % nb == 0 and t_total % tt == 0 and nb % V7X_SUBLANES == 0
    m = tt * nb
    n_chunks = t_total // tt
    n_steps = n_chunks + 1 if n_chunks > 1 else 1
    hr = HIST * nb
    gw = GROUP_WIDTH
    kern = functools.partial(_mixer_kernel, nb=nb, tt=tt, n_chunks=n_chunks, start_pos=start_pos,
                             layer=layer)
    stacked = lambda p: _layer_resident(p.shape, layer)
    out_blk = (lambda i: (jnp.maximum(i - 1, 0), 0)) if n_chunks > 1 else (lambda i: (0, 0))
    return pl.pallas_call(
        kern,
        grid=(n_steps,),
        in_specs=[pl.BlockSpec((m, D_MODEL), lambda i: (jnp.minimum(i, n_chunks - 1), 0)),
                  _layer_resident(h0.shape, state_layer), _layer_resident(cst.shape, state_layer),
                  _layer_resident(pst.shape, state_layer),
                  stacked(win), stacked(cw), _resident(cb.shape), stacked(gwt),
                  _resident(gab.shape), _resident(gxb.shape), _resident(lam.shape),
                  stacked(pw), _resident(ps.shape), stacked(wout), stacked(ln_g), stacked(ln_b)],
        out_specs=[pl.BlockSpec((m, D_MODEL), out_blk),
                   pl.BlockSpec((nb, gw), lambda i: (0, 0)),
                   pl.BlockSpec((nb, CONV_W - 1, gw), lambda i: (0, 0, 0)),
                   pl.BlockSpec((nb, POOL_MAX_WIN - 1, gw), lambda i: (0, 0, 0))],
        out_shape=[jax.ShapeDtypeStruct(x.shape, _F32),
                   jax.ShapeDtypeStruct((nb, gw), _F32),
                   jax.ShapeDtypeStruct((nb, CONV_W - 1, gw), _F32),
                   jax.ShapeDtypeStruct((nb, POOL_MAX_WIN - 1, gw), _F32)],
        scratch_shapes=[pltpu.VMEM((hr + m, gw), _F32),
                        pltpu.VMEM((hr + m, gw), _F32),
                        pltpu.VMEM((nb, gw), _F32),
                        pltpu.VMEM((m, gw), _F32),
                        pltpu.VMEM((m, gw), _F32),
                        pltpu.VMEM((m, gw), _F32),
                        pltpu.VMEM((m, gw), _F32),
                        pltpu.VMEM((m, gw), _F32),
                        pltpu.VMEM((m, D_MODEL), _F32)],
        compiler_params=pltpu.CompilerParams(
            dimension_semantics=("arbitrary",),
            vmem_limit_bytes=V7X_VMEM_LIMIT_BYTES),
        name="mixer",
    )(x, h0, cst, pst, win, cw, cb, gwt, gab, gxb, lam, pw, ps, wout, ln_g, ln_b)


def _block_diag_halves(w):
    depth, groups, n, _ = w.shape
    eye = jnp.eye(groups, dtype=w.dtype)
    full = (w[:, :, :, None, :] * eye[None, :, None, :, None]).reshape(
        depth, groups * n, groups * n)
    h = groups * n // 2
    return jnp.stack([full[:, :h, :h], full[:, h:, h:]], axis=1)


def kernel(x_prompt, x_sample, state_lru_h, state_conv, state_pool, ln_g, ln_b,
           w1_gate, w1_up, w1_down, w_in, conv_w, conv_b, gate_a_w, gate_a_b,
           gate_x_w, gate_x_b, lru_lambda, pool_w, pool_scale, w_out,
           w2_gate, w2_up, w2_down):
    batch, seq, d = x_prompt.shape
    dec_batch = x_sample.shape[0]
    gw = GROUP_WIDTH

    win = w_in.astype(_BF16)
    wout = w_out.astype(_BF16)
    gate_w = jnp.concatenate(
        [_block_diag_halves(gate_a_w), _block_diag_halves(gate_x_w)], axis=-1).astype(_BF16)
    pool_wb = _block_diag_halves(pool_w).astype(_BF16)
    mix_params = (win, conv_w, conv_b, gate_w, gate_a_b, gate_x_b, lru_lambda, pool_wb,
                  pool_scale, wout, ln_g, ln_b)

    xp = x_prompt
    xs = x_sample.reshape(dec_batch, d)
    zeros_h = jnp.zeros((1, batch, gw), _F32)
    zeros_conv = jnp.zeros((1, batch, CONV_W - 1, gw), _F32)
    zeros_pool = jnp.zeros((1, batch, POOL_MAX_WIN - 1, gw), _F32)

    p_h, p_conv, p_pool, s_h, s_conv, s_pool = [], [], [], [], [], []
    for l in range(DEPTH):
        xp, xs = _ffn(xp, xs, w1_gate, w1_up, w1_down, ln_g, ln_b, layer=l, ln_row=0,
                      layout="batch_to_time" if l == 0 else "rows")
        xp = xp.reshape(seq * batch, d)
        xp, h_t, cnew, pnew = _mixer(xp, zeros_h, zeros_conv, zeros_pool, *mix_params,
                                     layer=l, state_layer=0, nb=batch, start_pos=0)
        p_h.append(h_t); p_conv.append(cnew); p_pool.append(pnew)
        xs, h_t, cnew, pnew = _mixer(xs, state_lru_h, state_conv, state_pool, *mix_params,
                                     layer=l, state_layer=l, nb=dec_batch, start_pos=PAST_LEN)
        s_h.append(h_t); s_conv.append(cnew); s_pool.append(pnew)
        if l == DEPTH - 1:
            xp = xp.reshape(seq, batch, d)
        xp, xs = _ffn(xp, xs, w2_gate, w2_up, w2_down, ln_g, ln_b, layer=l, ln_row=2,
                      layout="time_to_batch" if l == DEPTH - 1 else "rows")

    y_sample = xs.reshape(dec_batch, 1, d)
    return (xp, y_sample,
            jnp.stack(p_h), jnp.stack(p_conv), jnp.stack(p_pool),
            jnp.stack(s_h), jnp.stack(s_conv), jnp.stack(s_pool))
```

```python
import functools

import jax
import jax.numpy as jnp
from jax import lax
from jax.experimental import pallas as pl
from jax.experimental.pallas import tpu as pltpu

D_MODEL = 1024
DEPTH = 4
GROUP_WIDTH = 512
LRU_C = 8.0
CONV_W = 4
POOL_WINDOWS = (2, 4, 8, 16)
POOL_MAX_WIN = 16
D_FF = 2816
LN_EPS = 1e-5
DEEPNORM_ALPHA = (2.0 * DEPTH) ** 0.25
PAST_LEN = 16384

V7X_LANES = 128
V7X_SUBLANES = 8
V7X_MXU_DIM = 256
V7X_VMEM_LIMIT_BYTES = 56 * 1024 * 1024

HIST = POOL_MAX_WIN
FFN_TOKEN_TILE = 512
MIXER_TIME_TILE = 64
FFN_CHUNKS = (768, 768, 768, 512)
assert sum(FFN_CHUNKS) == D_FF and all(c % V7X_MXU_DIM == 0 for c in FFN_CHUNKS)
FFN_LOAD_COLS = V7X_MXU_DIM
FFN_LOAD_SLOTS = 4
assert D_FF % FFN_LOAD_COLS == 0 and all(c % FFN_LOAD_COLS == 0 for c in FFN_CHUNKS)

_BF16 = jnp.bfloat16
_F32 = jnp.float32


def _dot(a, b):
    return jnp.dot(a, b, preferred_element_type=_F32)


def _layernorm(y, g, b):
    mu = jnp.mean(y, axis=-1, keepdims=True)
    d = y - mu
    var = jnp.mean(d * d, axis=-1, keepdims=True)
    return d * lax.rsqrt(var + LN_EPS) * g + b


def _zero_tile_from(v, shape):
    u = lax.bitcast_convert_type(v, jnp.uint32)
    rows, cols = u.shape
    bf16_rows = 2 * V7X_SUBLANES
    acc = u[0:bf16_rows, :]
    for k in range(1, rows // bf16_rows):
        acc = acc | u[k * bf16_rows:(k + 1) * bf16_rows, :]
    word = acc[:, 0:V7X_LANES]
    for k in range(1, cols // V7X_LANES):
        word = word | acc[:, k * V7X_LANES:(k + 1) * V7X_LANES]
    zero = lax.bitcast_convert_type((word >> 16) >> 16, _F32).astype(_BF16)
    zero = jnp.concatenate([zero] * (shape[1] // V7X_LANES), axis=1)
    return jnp.concatenate([zero] * (shape[0] // bf16_rows), axis=0)


def _resident(shape):
    nd = len(shape)
    return pl.BlockSpec(shape, lambda *_: (0,) * nd, pipeline_mode=pl.Buffered(1))


def _layer_resident(shape, layer):
    nd = len(shape)
    return pl.BlockSpec((None,) + tuple(shape[1:]), lambda *_: (layer,) + (0,) * (nd - 1),
                        pipeline_mode=pl.Buffered(1))


def _ffn_weight_loader(layer, wg_hbm, wu_hbm, wd_hbm, wg_v, wu_v, wd_v, st_gu, st_d, sems):
    n_pieces = D_FF // FFN_LOAD_COLS

    def copies(p):
        slot = p % FFN_LOAD_SLOTS
        cols = pl.ds(p * FFN_LOAD_COLS, FFN_LOAD_COLS)
        return (pltpu.make_async_copy(wg_hbm.at[layer, :, cols], st_gu.at[slot, 0], sems.at[slot, 0]),
                pltpu.make_async_copy(wu_hbm.at[layer, :, cols], st_gu.at[slot, 1], sems.at[slot, 1]),
                pltpu.make_async_copy(wd_hbm.at[layer, cols, :], st_d.at[slot], sems.at[slot, 2]))

    def start(p):
        if p < n_pieces:
            for cp in copies(p):
                cp.start()

    def finish(p):
        slot = p % FFN_LOAD_SLOTS
        for cp in copies(p):
            cp.wait()
        lo = p * FFN_LOAD_COLS
        wg_v[:, lo:lo + FFN_LOAD_COLS] = st_gu[slot, 0].astype(_BF16)
        wu_v[:, lo:lo + FFN_LOAD_COLS] = st_gu[slot, 1].astype(_BF16)
        wd_v[lo:lo + FFN_LOAD_COLS, :] = st_d[slot].astype(_BF16)

    return start, finish


def _swiglu(xb, wg_v, wu_v, wd_v, per_chunk=None, before_chunk=None):
    acc = jnp.zeros((xb.shape[0], D_MODEL), _F32)
    lo = 0
    for ci, width in enumerate(FFN_CHUNKS):
        if before_chunk is not None:
            before_chunk(lo, width)
        gate = _dot(xb, wg_v[:, lo:lo + width])
        up = _dot(xb, wu_v[:, lo:lo + width])
        hid = (jax.nn.silu(gate) * up).astype(_BF16)
        if per_chunk is not None:
            hid = per_chunk(ci, hid)
        acc = acc + _dot(hid, wd_v[lo:lo + width, :])
        lo += width
    return acc


def _ffn_kernel(xp_ref, xs_ref, wg_hbm, wu_hbm, wd_hbm, g_ref, b_ref, op_ref, os_ref,
                wg_v, wu_v, wd_v, st_gu, st_d, sems, y_scr, ys_scr,
                *, layer, ln_row, n_tiles, layout):
    s = pl.program_id(0)
    ln = lambda y: _layernorm(y, g_ref[ln_row:ln_row + 1, :], b_ref[ln_row:ln_row + 1, :])

    if layout == "rows":
        load_tile = lambda: xp_ref[...]
        def store_rows(r0, r1, val):
            op_ref[r0:r1, :] = val
    elif layout == "batch_to_time":
        nbat, tt = xp_ref.shape[0], xp_ref.shape[1]
        load_tile = lambda: xp_ref[...].reshape(nbat * tt, D_MODEL)
        def store_rows(r0, r1, val):
            for b in range(r0 // tt, r1 // tt):
                op_ref[:, b, :] = val[b * tt - r0:(b + 1) * tt - r0, :]
    else:
        tt, nbat = xp_ref.shape[0], xp_ref.shape[1]
        load_tile = lambda: jnp.concatenate([xp_ref[:, b, :] for b in range(nbat)], axis=0)
        def store_rows(r0, r1, val):
            op_ref[r0 // tt:r1 // tt, :, :] = val.reshape((r1 - r0) // tt, tt, D_MODEL)

    def prompt_tile(before_chunk=None):
        x = load_tile()
        part = x.shape[0] // len(FFN_CHUNKS)

        def ln_part(ci, hid):
            rows = slice(ci * part, (ci + 1) * part)
            normed = ln(y_scr[rows, :])
            store_rows(ci * part, (ci + 1) * part, normed)
            return hid + _zero_tile_from(normed, hid.shape)

        acc = _swiglu(x.astype(_BF16), wg_v, wu_v, wd_v, ln_part, before_chunk)
        y_scr[...] = DEEPNORM_ALPHA * x + 0.5 * acc

    @pl.when(s == 0)
    def _():
        y_scr[...] = jnp.zeros(y_scr.shape, _F32)
        start, finish = _ffn_weight_loader(layer, wg_hbm, wu_hbm, wd_hbm, wg_v, wu_v, wd_v,
                                           st_gu, st_d, sems)
        for p in range(FFN_LOAD_SLOTS):
            start(p)

        def load_chunk(lo, width):
            for p in range(lo // FFN_LOAD_COLS, (lo + width) // FFN_LOAD_COLS):
                finish(p)
                start(p + FFN_LOAD_SLOTS)

        prompt_tile(load_chunk)

    @pl.when((s > 0) & (s < n_tiles))
    def _():
        prompt_tile()

    @pl.when(s == n_tiles)
    def _():
        store_rows(0, y_scr.shape[0], ln(y_scr[...]))
        xs = xs_ref[...]
        ys_scr[...] = DEEPNORM_ALPHA * xs + 0.5 * _swiglu(xs.astype(_BF16), wg_v, wu_v, wd_v)

    @pl.when(s == n_tiles + 1)
    def _():
        os_ref[...] = ln(ys_scr[...])


def _ffn(xp, xs, wg, wu, wd, ln_g, ln_b, *, layer, ln_row, layout="rows"):
    ms = xs.shape[0]
    tm = FFN_TOKEN_TILE
    assert tm % (len(FFN_CHUNKS) * 2 * V7X_SUBLANES) == 0
    if layout == "rows":
        m = xp.shape[0]
        in_block, out_block, out_full = (tm, D_MODEL), (tm, D_MODEL), (m, D_MODEL)
        at = lambda j: (j, 0)
        in_at = out_at = at
    else:
        nbat, t_total = (xp.shape[0], xp.shape[1]) if layout == "batch_to_time" else (xp.shape[1], xp.shape[0])
        m = nbat * t_total
        tt = tm // nbat
        part_rows = tm // len(FFN_CHUNKS)
        assert tm % nbat == 0 and tt % V7X_SUBLANES == 0 and part_rows % tt == 0
        batch_major = ((nbat, tt, D_MODEL), lambda j: (0, j, 0), (nbat, t_total, D_MODEL))
        time_major = ((tt, nbat, D_MODEL), lambda j: (j, 0, 0), (t_total, nbat, D_MODEL))
        src, dst = (batch_major, time_major) if layout == "batch_to_time" else (time_major, batch_major)
        in_block, in_at = src[0], src[1]
        out_block, out_at, out_full = dst
    assert m % tm == 0
    n_tiles = m // tm
    hbm = pl.BlockSpec(memory_space=pl.ANY)
    return pl.pallas_call(
        functools.partial(_ffn_kernel, layer=layer, ln_row=ln_row, n_tiles=n_tiles, layout=layout),
        grid=(n_tiles + 2,),
        in_specs=[
            pl.BlockSpec(in_block, lambda i: in_at(jnp.minimum(i, n_tiles - 1))),
            pl.BlockSpec((ms, D_MODEL), lambda i: (0, 0)),
            hbm, hbm, hbm,
            _layer_resident(ln_g.shape, layer), _layer_resident(ln_b.shape, layer),
        ],
        out_specs=[pl.BlockSpec(out_block, lambda i: out_at(jnp.clip(i - 1, 0, n_tiles - 1))),
                   pl.BlockSpec((ms, D_MODEL), lambda i: (0, 0))],
        out_shape=[jax.ShapeDtypeStruct(out_full, _F32),
                   jax.ShapeDtypeStruct((ms, D_MODEL), _F32)],
        scratch_shapes=[pltpu.VMEM((D_MODEL, D_FF), _BF16),
                        pltpu.VMEM((D_MODEL, D_FF), _BF16),
                        pltpu.VMEM((D_FF, D_MODEL), _BF16),
                        pltpu.VMEM((FFN_LOAD_SLOTS, 2, D_MODEL, FFN_LOAD_COLS), _F32),
                        pltpu.VMEM((FFN_LOAD_SLOTS, FFN_LOAD_COLS, D_MODEL), _F32),
                        pltpu.SemaphoreType.DMA((FFN_LOAD_SLOTS, 3)),
                        pltpu.VMEM((tm, D_MODEL), _F32),
                        pltpu.VMEM((ms, D_MODEL), _F32)],
        compiler_params=pltpu.CompilerParams(
            dimension_semantics=("arbitrary",),
            vmem_limit_bytes=V7X_VMEM_LIMIT_BYTES),
        name="ffn",
    )(xp, xs, wg, wu, wd, ln_g, ln_b)


def _mixer_kernel(x_ref, h0_ref, cst_ref, pst_ref, win_ref, cw_ref, cb_ref, gw_ref, gab_ref,
                  gxb_ref, lam_ref, pw_ref, ps_ref, wout_ref, g_ref, b_ref,
                  o_ref, ht_ref, cnew_ref, pnew_ref,
                  ul_ext, up_ext, h_scr, ga_scr, gx_scr, xc_scr, hs_scr, gate_scr, x_scr,
                  *, nb, tt, n_chunks, start_pos, layer):
    s = pl.program_id(0)
    pipelined = n_chunks > 1
    m = tt * nb
    hr = HIST * nb
    gw = GROUP_WIDTH
    half = gw // 2
    vec = lambda ref: ref[layer:layer + 1, :]

    @pl.when(s == 0)
    def _():
        h_scr[...] = h0_ref[...]
        ul_ext[0:hr, :] = jnp.zeros((hr, gw), _F32)
        for k in range(CONV_W - 1):
            j = HIST - (CONV_W - 1) + k
            ul_ext[j * nb:(j + 1) * nb, :] = cst_ref[:, k, :]
        if pipelined:
            for ref in (ga_scr, gx_scr, xc_scr, gate_scr, x_scr):
                ref[...] = jnp.zeros(ref.shape, _F32)
            up_ext[0:m, :] = jnp.zeros((m, gw), _F32)
        pool0 = m if pipelined else 0
        up_ext[pool0:pool0 + nb, :] = jnp.zeros((nb, gw), _F32)
        for k in range(POOL_MAX_WIN - 1):
            j = HIST - (POOL_MAX_WIN - 1) + k
            up_ext[pool0 + j * nb:pool0 + (j + 1) * nb, :] = pst_ref[:, k, :]

    def project():
        x = x_ref[...]
        return x, _dot(x.astype(_BF16), win_ref[...])

    def conv_and_gates(proj):
        u_lru = proj[:, 0:gw]
        gate_scr[...] = proj[:, gw:2 * gw]
        ul_ext[hr:hr + m, :] = u_lru
        up_ext[hr:hr + m, :] = proj[:, 2 * gw:3 * gw]
        xc = vec(cb_ref) + cw_ref[CONV_W - 1:CONV_W, :] * u_lru
        for k in range(CONV_W - 1):
            off = hr - (CONV_W - 1 - k) * nb
            xc = xc + cw_ref[k:k + 1, :] * ul_ext[off:off + m, :]
        xc_scr[...] = xc
        xcb = xc.astype(_BF16)
        for p in range(2):
            cols = slice(p * half, (p + 1) * half)
            gts = _dot(xcb[:, cols], gw_ref[p])
            ga_scr[:, cols] = gts[:, 0:half]
            gx_scr[:, cols] = gts[:, half:2 * half]
        ul_ext[0:hr, :] = ul_ext[m:m + hr, :]

    def recurrence_and_mix(chunk, live):
        xc = xc_scr[...]
        r = jax.nn.sigmoid(ga_scr[...] + vec(gab_ref))
        i = jax.nn.sigmoid(gx_scr[...] + vec(gxb_ref))
        nl = -vec(lam_ref)
        softplus_nl = jnp.maximum(nl, 0.0) + jnp.log1p(jnp.exp(-jnp.abs(nl)))
        log_a = -LRU_C * r * softplus_nl
        a = jnp.exp(log_a)
        mult = jnp.sqrt(1.0 - a * a)
        uu = mult * (i * xc)
        h_in = h_scr[...]
        h = h_in
        for t in range(tt):
            rows = slice(t * nb, (t + 1) * nb)
            h = a[rows, :] * h + uu[rows, :]
            hs_scr[rows, :] = h
        if live is not None:
            h = jnp.where(live, h, h_in)
        h_scr[...] = h
        ht_ref[...] = h
        y_lru = hs_scr[...] * jax.nn.gelu(gate_scr[...])

        t_loc = lax.broadcasted_iota(jnp.int32, (tt, nb, V7X_LANES), 0).reshape(m, V7X_LANES)
        pos = start_pos + chunk * tt + t_loc
        pooled = []
        for gi, w in enumerate(POOL_WINDOWS):
            lanes = slice(gi * V7X_LANES, (gi + 1) * V7X_LANES)
            n_rows = (w - 1) * nb + m
            sums = up_ext[hr - (w - 1) * nb:hr + m, lanes]
            span = 1
            while span < w:
                n_rows -= span * nb
                sums = sums[span * nb:span * nb + n_rows, :] + sums[0:n_rows, :]
                span *= 2
            cnt = jnp.minimum(w, pos + 1).astype(_F32)
            pooled.append(sums / cnt - up_ext[hr:hr + m, lanes])
        pooled = jnp.concatenate(pooled, axis=-1).astype(_BF16)
        y_pool = jnp.concatenate(
            [_dot(pooled[:, p * half:(p + 1) * half], pw_ref[p]) for p in range(2)],
            axis=-1) * vec(ps_ref)

        mix = _dot(jnp.concatenate([y_lru, y_pool], axis=-1).astype(_BF16), wout_ref[...])
        up_ext[0:hr, :] = up_ext[m:m + hr, :]
        return mix

    def residual_norm(mix):
        y = DEEPNORM_ALPHA * x_scr[...] + mix
        o_ref[...] = _layernorm(y, g_ref[1:2, :], b_ref[1:2, :])

    x, proj = project()
    if pipelined:
        mix = recurrence_and_mix(jnp.maximum(s - 1, 0), s > 0)
        conv_and_gates(proj)
        residual_norm(mix)
        x_scr[...] = x
    else:
        x_scr[...] = x
        conv_and_gates(proj)
        residual_norm(recurrence_and_mix(s, None))

    @pl.when(s == n_chunks - 1)
    def _():
        for k in range(CONV_W - 1):
            j = HIST - (CONV_W - 1) + k
            cnew_ref[:, k, :] = ul_ext[j * nb:(j + 1) * nb, :]

    @pl.when(s == pl.num_programs(0) - 1)
    def _():
        for k in range(POOL_MAX_WIN - 1):
            j = HIST - (POOL_MAX_WIN - 1) + k
            pnew_ref[:, k, :] = up_ext[j * nb:(j + 1) * nb, :]


def _mixer(x, h0, cst, pst, win, cw, cb, gwt, gab, gxb, lam, pw, ps, wout, ln_g, ln_b,
           *, layer, state_layer, nb, start_pos):
    n_rows = x.shape[0]
    t_total = n_rows // nb
    tt = min(MIXER_TIME_TILE, t_total)
    assert n_rows % nb == 0 and t_total % tt == 0 and nb % V7X_SUBLANES == 0
    m = tt * nb
    n_chunks = t_total // tt
    n_steps = n_chunks + 1 if n_chunks > 1 else 1
    hr = HIST * nb
    gw = GROUP_WIDTH
    kern = functools.partial(_mixer_kernel, nb=nb, tt=tt, n_chunks=n_chunks, start_pos=start_pos,
                             layer=layer)
    stacked = lambda p: _layer_resident(p.shape, layer)
    out_blk = (lambda i: (jnp.maximum(i - 1, 0), 0)) if n_chunks > 1 else (lambda i: (0, 0))
    return pl.pallas_call(
        kern,
        grid=(n_steps,),
        in_specs=[pl.BlockSpec((m, D_MODEL), lambda i: (jnp.minimum(i, n_chunks - 1), 0)),
                  _layer_resident(h0.shape, state_layer), _layer_resident(cst.shape, state_layer),
                  _layer_resident(pst.shape, state_layer),
                  stacked(win), stacked(cw), _resident(cb.shape), stacked(gwt),
                  _resident(gab.shape), _resident(gxb.shape), _resident(lam.shape),
                  stacked(pw), _resident(ps.shape), stacked(wout), stacked(ln_g), stacked(ln_b)],
        out_specs=[pl.BlockSpec((m, D_MODEL), out_blk),
                   pl.BlockSpec((nb, gw), lambda i: (0, 0)),
                   pl.BlockSpec((nb, CONV_W - 1, gw), lambda i: (0, 0, 0)),
                   pl.BlockSpec((nb, POOL_MAX_WIN - 1, gw), lambda i: (0, 0, 0))],
        out_shape=[jax.ShapeDtypeStruct(x.shape, _F32),
                   jax.ShapeDtypeStruct((nb, gw), _F32),
                   jax.ShapeDtypeStruct((nb, CONV_W - 1, gw), _F32),
                   jax.ShapeDtypeStruct((nb, POOL_MAX_WIN - 1, gw), _F32)],
        scratch_shapes=[pltpu.VMEM((hr + m, gw), _F32),
                        pltpu.VMEM((hr + m, gw), _F32),
                        pltpu.VMEM((nb, gw), _F32),
                        pltpu.VMEM((m, gw), _F32),
                        pltpu.VMEM((m, gw), _F32),
                        pltpu.VMEM((m, gw), _F32),
                        pltpu.VMEM((m, gw), _F32),
                        pltpu.VMEM((m, gw), _F32),
                        pltpu.VMEM((m, D_MODEL), _F32)],
        compiler_params=pltpu.CompilerParams(
            dimension_semantics=("arbitrary",),
            vmem_limit_bytes=V7X_VMEM_LIMIT_BYTES),
        name="mixer",
    )(x, h0, cst, pst, win, cw, cb, gwt, gab, gxb, lam, pw, ps, wout, ln_g, ln_b)


def _block_diag_halves(w):
    groups, n = w.shape[1], w.shape[2]
    gh = groups // 2
    halves = []
    for p in range(2):
        rows = [jnp.pad(w[:, p * gh + g], ((0, 0), (0, 0), (g * n, (gh - 1 - g) * n)))
                for g in range(gh)]
        halves.append(jnp.concatenate(rows, axis=1))
    return jnp.stack(halves, axis=1)


def kernel(x_prompt, x_sample, state_lru_h, state_conv, state_pool, ln_g, ln_b,
           w1_gate, w1_up, w1_down, w_in, conv_w, conv_b, gate_a_w, gate_a_b,
           gate_x_w, gate_x_b, lru_lambda, pool_w, pool_scale, w_out,
           w2_gate, w2_up, w2_down):
    batch, seq, d = x_prompt.shape
    dec_batch = x_sample.shape[0]
    gw = GROUP_WIDTH

    win = w_in.astype(_BF16)
    wout = w_out.astype(_BF16)
    gate_w = jnp.concatenate(
        [_block_diag_halves(gate_a_w), _block_diag_halves(gate_x_w)], axis=-1).astype(_BF16)
    pool_wb = _block_diag_halves(pool_w).astype(_BF16)
    mix_params = (win, conv_w, conv_b, gate_w, gate_a_b, gate_x_b, lru_lambda, pool_wb,
                  pool_scale, wout, ln_g, ln_b)

    xp = x_prompt
    xs = x_sample.reshape(dec_batch, d)
    zeros_h = jnp.zeros((1, batch, gw), _F32)
    zeros_conv = jnp.zeros((1, batch, CONV_W - 1, gw), _F32)
    zeros_pool = jnp.zeros((1, batch, POOL_MAX_WIN - 1, gw), _F32)

    p_h, p_conv, p_pool, s_h, s_conv, s_pool = [], [], [], [], [], []
    for l in range(DEPTH):
        xp, xs = _ffn(xp, xs, w1_gate, w1_up, w1_down, ln_g, ln_b, layer=l, ln_row=0,
                      layout="batch_to_time" if l == 0 else "rows")
        xp = xp.reshape(seq * batch, d)
        xp, h_t, cnew, pnew = _mixer(xp, zeros_h, zeros_conv, zeros_pool, *mix_params,
                                     layer=l, state_layer=0, nb=batch, start_pos=0)
        p_h.append(h_t); p_conv.append(cnew); p_pool.append(pnew)
        xs, h_t, cnew, pnew = _mixer(xs, state_lru_h, state_conv, state_pool, *mix_params,
                                     layer=l, state_layer=l, nb=dec_batch, start_pos=PAST_LEN)
        s_h.append(h_t); s_conv.append(cnew); s_pool.append(pnew)
        if l == DEPTH - 1:
            xp = xp.reshape(seq, batch, d)
        xp, xs = _ffn(xp, xs, w2_gate, w2_up, w2_down, ln_g, ln_b, layer=l, ln_row=2,
                      layout="time_to_batch" if l == DEPTH - 1 else "rows")

    y_sample = xs.reshape(dec_batch, 1, d)
    return (xp, y_sample,
            jnp.stack(p_h), jnp.stack(p_conv), jnp.stack(p_pool),
            jnp.stack(s_h), jnp.stack(s_conv), jnp.stack(s_pool))
```

```python
import functools
import math

import jax
import jax.numpy as jnp
from jax import lax
from jax.experimental import pallas as pl
from jax.experimental.pallas import tpu as pltpu

D_MODEL = 1024
DEPTH = 4
GROUP_WIDTH = 512
LRU_C = 8.0
CONV_W = 4
POOL_WINDOWS = (2, 4, 8, 16)
POOL_MAX_WIN = 16
D_FF = 2816
LN_EPS = 1e-5
DEEPNORM_ALPHA = (2.0 * DEPTH) ** 0.25
PAST_LEN = 16384
assert math.frexp(LRU_C)[0] == 0.5 and all(math.frexp(w)[0] == 0.5 for w in POOL_WINDOWS)

V7X_LANES = 128
V7X_SUBLANES = 8
V7X_MXU_DIM = 256
V7X_VMEM_LIMIT_BYTES = 56 * 1024 * 1024

HIST = POOL_MAX_WIN
FFN_TOKEN_TILE = 512
MIXER_TIME_TILE = 64
FFN_CHUNKS = (768, 768, 768, 512)
assert sum(FFN_CHUNKS) == D_FF and all(c % V7X_MXU_DIM == 0 for c in FFN_CHUNKS)
FFN_LOAD_COLS = V7X_MXU_DIM
FFN_LOAD_SLOTS = 4
assert D_FF % FFN_LOAD_COLS == 0 and all(c % FFN_LOAD_COLS == 0 for c in FFN_CHUNKS)

_BF16 = jnp.bfloat16
_F32 = jnp.float32


def _dot(a, b):
    return jnp.dot(a, b, preferred_element_type=_F32)


def _layernorm(y, g, b):
    mu = jnp.mean(y, axis=-1, keepdims=True)
    d = y - mu
    var = jnp.mean(d * d, axis=-1, keepdims=True)
    return d * lax.rsqrt(var + LN_EPS) * g + b


def _zero_tile_from(v, shape):
    u = lax.bitcast_convert_type(v, jnp.uint32)
    rows, cols = u.shape
    bf16_rows = 2 * V7X_SUBLANES
    acc = u[0:bf16_rows, :]
    for k in range(1, rows // bf16_rows):
        acc = acc | u[k * bf16_rows:(k + 1) * bf16_rows, :]
    word = acc[:, 0:V7X_LANES]
    for k in range(1, cols // V7X_LANES):
        word = word | acc[:, k * V7X_LANES:(k + 1) * V7X_LANES]
    zero = lax.bitcast_convert_type((word >> 16) >> 16, _F32).astype(_BF16)
    zero = jnp.concatenate([zero] * (shape[1] // V7X_LANES), axis=1)
    return jnp.concatenate([zero] * (shape[0] // bf16_rows), axis=0)


def _resident(shape):
    nd = len(shape)
    return pl.BlockSpec(shape, lambda *_: (0,) * nd, pipeline_mode=pl.Buffered(1))


def _layer_resident(shape, layer):
    nd = len(shape)
    return pl.BlockSpec((None,) + tuple(shape[1:]), lambda *_: (layer,) + (0,) * (nd - 1),
                        pipeline_mode=pl.Buffered(1))


def _ffn_weight_loader(layer, wg_hbm, wu_hbm, wd_hbm, wg_v, wu_v, wd_v, st_gu, st_d, sems):
    n_pieces = D_FF // FFN_LOAD_COLS

    def copies(p):
        slot = p % FFN_LOAD_SLOTS
        cols = pl.ds(p * FFN_LOAD_COLS, FFN_LOAD_COLS)
        return (pltpu.make_async_copy(wg_hbm.at[layer, :, cols], st_gu.at[slot, 0], sems.at[slot, 0]),
                pltpu.make_async_copy(wu_hbm.at[layer, :, cols], st_gu.at[slot, 1], sems.at[slot, 1]),
                pltpu.make_async_copy(wd_hbm.at[layer, cols, :], st_d.at[slot], sems.at[slot, 2]))

    def start(p):
        if p < n_pieces:
            for cp in copies(p):
                cp.start()

    def finish(p):
        slot = p % FFN_LOAD_SLOTS
        for cp in copies(p):
            cp.wait()
        lo = p * FFN_LOAD_COLS
        wg_v[:, lo:lo + FFN_LOAD_COLS] = st_gu[slot, 0].astype(_BF16)
        wu_v[:, lo:lo + FFN_LOAD_COLS] = st_gu[slot, 1].astype(_BF16)
        wd_v[lo:lo + FFN_LOAD_COLS, :] = st_d[slot].astype(_BF16)

    return start, finish


def _swiglu(xb, wg_v, wu_v, wd_v, per_chunk=None, before_chunk=None):
    acc = jnp.zeros((xb.shape[0], D_MODEL), _F32)
    lo = 0
    for ci, width in enumerate(FFN_CHUNKS):
        if before_chunk is not None:
            before_chunk(lo, width)
        gate = _dot(xb, wg_v[:, lo:lo + width])
        up = _dot(xb, wu_v[:, lo:lo + width])
        hid = (jax.nn.silu(gate) * up).astype(_BF16)
        if per_chunk is not None:
            hid = per_chunk(ci, hid)
        acc = acc + _dot(hid, wd_v[lo:lo + width, :])
        lo += width
    return acc


def _ffn_kernel(xp_ref, xs_ref, wg_hbm, wu_hbm, wd_hbm, g_ref, b_ref, op_ref, os_ref,
                wg_v, wu_v, wd_v, st_gu, st_d, sems, y_scr, ys_scr,
                *, layer, ln_row, n_tiles, layout):
    s = pl.program_id(0)
    ln = lambda y: _layernorm(y, g_ref[ln_row:ln_row + 1, :], b_ref[ln_row:ln_row + 1, :])

    if layout == "rows":
        load_tile = lambda: xp_ref[...]
        def store_rows(r0, r1, val):
            op_ref[r0:r1, :] = val
    elif layout == "batch_to_time":
        nbat, tt = xp_ref.shape[0], xp_ref.shape[1]
        load_tile = lambda: xp_ref[...].reshape(nbat * tt, D_MODEL)
        def store_rows(r0, r1, val):
            for b in range(r0 // tt, r1 // tt):
                op_ref[:, b, :] = val[b * tt - r0:(b + 1) * tt - r0, :]
    else:
        tt, nbat = xp_ref.shape[0], xp_ref.shape[1]
        load_tile = lambda: jnp.concatenate([xp_ref[:, b, :] for b in range(nbat)], axis=0)
        def store_rows(r0, r1, val):
            op_ref[r0 // tt:r1 // tt, :, :] = val.reshape((r1 - r0) // tt, tt, D_MODEL)

    def prompt_tile(before_chunk=None):
        x = load_tile()
        part = x.shape[0] // len(FFN_CHUNKS)

        def ln_part(ci, hid):
            rows = slice(ci * part, (ci + 1) * part)
            normed = ln(y_scr[rows, :])
            store_rows(ci * part, (ci + 1) * part, normed)
            return hid + _zero_tile_from(normed, hid.shape)

        acc = _swiglu(x.astype(_BF16), wg_v, wu_v, wd_v, ln_part, before_chunk)
        y_scr[...] = DEEPNORM_ALPHA * x + 0.5 * acc

    @pl.when(s == 0)
    def _():
        y_scr[...] = jnp.zeros(y_scr.shape, _F32)
        start, finish = _ffn_weight_loader(layer, wg_hbm, wu_hbm, wd_hbm, wg_v, wu_v, wd_v,
                                           st_gu, st_d, sems)
        for p in range(FFN_LOAD_SLOTS):
            start(p)

        def load_chunk(lo, width):
            for p in range(lo // FFN_LOAD_COLS, (lo + width) // FFN_LOAD_COLS):
                finish(p)
                start(p + FFN_LOAD_SLOTS)

        prompt_tile(load_chunk)

    @pl.when((s > 0) & (s < n_tiles))
    def _():
        prompt_tile()

    @pl.when(s == n_tiles)
    def _():
        store_rows(0, y_scr.shape[0], ln(y_scr[...]))
        xs = xs_ref[...]
        ys_scr[...] = DEEPNORM_ALPHA * xs + 0.5 * _swiglu(xs.astype(_BF16), wg_v, wu_v, wd_v)

    @pl.when(s == n_tiles + 1)
    def _():
        os_ref[...] = ln(ys_scr[...])


def _ffn(xp, xs, wg, wu, wd, ln_g, ln_b, *, layer, ln_row, layout="rows"):
    ms = xs.shape[0]
    tm = FFN_TOKEN_TILE
    assert tm % (len(FFN_CHUNKS) * 2 * V7X_SUBLANES) == 0
    if layout == "rows":
        m = xp.shape[0]
        in_block, out_block, out_full = (tm, D_MODEL), (tm, D_MODEL), (m, D_MODEL)
        at = lambda j: (j, 0)
        in_at = out_at = at
    else:
        nbat, t_total = (xp.shape[0], xp.shape[1]) if layout == "batch_to_time" else (xp.shape[1], xp.shape[0])
        m = nbat * t_total
        tt = tm // nbat
        part_rows = tm // len(FFN_CHUNKS)
        assert tm % nbat == 0 and tt % V7X_SUBLANES == 0 and part_rows % tt == 0
        batch_major = ((nbat, tt, D_MODEL), lambda j: (0, j, 0), (nbat, t_total, D_MODEL))
        time_major = ((tt, nbat, D_MODEL), lambda j: (j, 0, 0), (t_total, nbat, D_MODEL))
        src, dst = (batch_major, time_major) if layout == "batch_to_time" else (time_major, batch_major)
        in_block, in_at = src[0], src[1]
        out_block, out_at, out_full = dst
    assert m % tm == 0
    n_tiles = m // tm
    hbm = pl.BlockSpec(memory_space=pl.ANY)
    return pl.pallas_call(
        functools.partial(_ffn_kernel, layer=layer, ln_row=ln_row, n_tiles=n_tiles, layout=layout),
        grid=(n_tiles + 2,),
        in_specs=[
            pl.BlockSpec(in_block, lambda i: in_at(jnp.minimum(i, n_tiles - 1))),
            pl.BlockSpec((ms, D_MODEL), lambda i: (0, 0)),
            hbm, hbm, hbm,
            _layer_resident(ln_g.shape, layer), _layer_resident(ln_b.shape, layer),
        ],
        out_specs=[pl.BlockSpec(out_block, lambda i: out_at(jnp.clip(i - 1, 0, n_tiles - 1))),
                   pl.BlockSpec((ms, D_MODEL), lambda i: (0, 0))],
        out_shape=[jax.ShapeDtypeStruct(out_full, _F32),
                   jax.ShapeDtypeStruct((ms, D_MODEL), _F32)],
        scratch_shapes=[pltpu.VMEM((D_MODEL, D_FF), _BF16),
                        pltpu.VMEM((D_MODEL, D_FF), _BF16),
                        pltpu.VMEM((D_FF, D_MODEL), _BF16),
                        pltpu.VMEM((FFN_LOAD_SLOTS, 2, D_MODEL, FFN_LOAD_COLS), _F32),
                        pltpu.VMEM((FFN_LOAD_SLOTS, FFN_LOAD_COLS, D_MODEL), _F32),
                        pltpu.SemaphoreType.DMA((FFN_LOAD_SLOTS, 3)),
                        pltpu.VMEM((tm, D_MODEL), _F32),
                        pltpu.VMEM((ms, D_MODEL), _F32)],
        compiler_params=pltpu.CompilerParams(
            dimension_semantics=("arbitrary",),
            vmem_limit_bytes=V7X_VMEM_LIMIT_BYTES),
        name="ffn",
    )(xp, xs, wg, wu, wd, ln_g, ln_b)


def _mixer_kernel(x_ref, h0_ref, cst_ref, pst_ref, win_ref, cw_ref, cb_ref, gw_ref, gab_ref,
                  gxb_ref, lam_ref, pw_ref, ps_ref, wout_ref, g_ref, b_ref,
                  o_ref, ht_ref, cnew_ref, pnew_ref,
                  ul_ext, up_ext, h_scr, ga_scr, gx_scr, xc_scr, hs_scr, gate_scr, x_scr,
                  *, nb, tt, n_chunks, start_pos, layer):
    s = pl.program_id(0)
    pipelined = n_chunks > 1
    m = tt * nb
    hr = HIST * nb
    gw = GROUP_WIDTH
    half = gw // 2
    vec = lambda ref: ref[layer:layer + 1, :]

    @pl.when(s == 0)
    def _():
        h_scr[...] = h0_ref[...]
        ul_ext[0:hr, :] = jnp.zeros((hr, gw), _F32)
        for k in range(CONV_W - 1):
            j = HIST - (CONV_W - 1) + k
            ul_ext[j * nb:(j + 1) * nb, :] = cst_ref[:, k, :]
        if pipelined:
            for ref in (ga_scr, gx_scr, xc_scr, gate_scr, x_scr):
                ref[...] = jnp.zeros(ref.shape, _F32)
            up_ext[0:m, :] = jnp.zeros((m, gw), _F32)
        pool0 = m if pipelined else 0
        up_ext[pool0:pool0 + nb, :] = jnp.zeros((nb, gw), _F32)
        for k in range(POOL_MAX_WIN - 1):
            j = HIST - (POOL_MAX_WIN - 1) + k
            up_ext[pool0 + j * nb:pool0 + (j + 1) * nb, :] = pst_ref[:, k, :]

    def project():
        x = x_ref[...]
        return x, _dot(x.astype(_BF16), win_ref[...])

    def conv_and_gates(proj):
        u_lru = proj[:, 0:gw]
        gate_scr[...] = proj[:, gw:2 * gw]
        ul_ext[hr:hr + m, :] = u_lru
        up_ext[hr:hr + m, :] = proj[:, 2 * gw:3 * gw]
        xc = vec(cb_ref) + cw_ref[CONV_W - 1:CONV_W, :] * u_lru
        for k in range(CONV_W - 1):
            off = hr - (CONV_W - 1 - k) * nb
            xc = xc + cw_ref[k:k + 1, :] * ul_ext[off:off + m, :]
        xc_scr[...] = xc
        xcb = xc.astype(_BF16)
        for p in range(2):
            cols = slice(p * half, (p + 1) * half)
            gts = _dot(xcb[:, cols], gw_ref[p])
            ga_scr[:, cols] = gts[:, 0:half]
            gx_scr[:, cols] = gts[:, half:2 * half]
        ul_ext[0:hr, :] = ul_ext[m:m + hr, :]

    def recurrence_and_mix(chunk, live):
        xc = xc_scr[...]
        r = jax.nn.sigmoid(ga_scr[...] + vec(gab_ref))
        i = jax.nn.sigmoid(gx_scr[...] + vec(gxb_ref))
        nl = -vec(lam_ref)
        softplus_nl = jnp.maximum(nl, 0.0) + jnp.log1p(jnp.exp(-jnp.abs(nl)))
        log_a = r * (-LRU_C * softplus_nl)
        a = jnp.exp(log_a)
        one_m_a2 = 1.0 - a * a
        mult = jnp.where(one_m_a2 > 0.0, one_m_a2 * lax.rsqrt(one_m_a2), one_m_a2)
        uu = mult * (i * xc)
        h_in = h_scr[...]
        h = h_in
        for t in range(tt):
            rows = slice(t * nb, (t + 1) * nb)
            h = a[rows, :] * h + uu[rows, :]
            hs_scr[rows, :] = h
        if live is not None:
            h = jnp.where(live, h, h_in)
        h_scr[...] = h
        ht_ref[...] = h
        y_lru = hs_scr[...] * jax.nn.gelu(gate_scr[...])

        t_loc = lax.broadcasted_iota(jnp.int32, (tt, nb, V7X_LANES), 0).reshape(m, V7X_LANES)
        pos = start_pos + chunk * tt + t_loc
        pooled = []
        for gi, w in enumerate(POOL_WINDOWS):
            lanes = slice(gi * V7X_LANES, (gi + 1) * V7X_LANES)
            n_rows = (w - 1) * nb + m
            sums = up_ext[hr - (w - 1) * nb:hr + m, lanes]
            span = 1
            while span < w:
                n_rows -= span * nb
                sums = sums[span * nb:span * nb + n_rows, :] + sums[0:n_rows, :]
                span *= 2
            head = min(m, (w - 1) * nb)
            cnt = jnp.minimum(w, pos[0:head, :] + 1).astype(_F32)
            mean = sums[0:head, :] / cnt
            if head < m:
                mean = jnp.concatenate([mean, sums[head:m, :] * (1.0 / w)], axis=0)
            pooled.append(mean - up_ext[hr:hr + m, lanes])
        pooled = jnp.concatenate(pooled, axis=-1).astype(_BF16)
        y_pool = jnp.concatenate(
            [_dot(pooled[:, p * half:(p + 1) * half], pw_ref[p]) for p in range(2)],
            axis=-1) * vec(ps_ref)

        mix = _dot(jnp.concatenate([y_lru, y_pool], axis=-1).astype(_BF16), wout_ref[...])
        up_ext[0:hr, :] = up_ext[m:m + hr, :]
        return mix

    def residual_norm(mix):
        y = DEEPNORM_ALPHA * x_scr[...] + mix
        o_ref[...] = _layernorm(y, g_ref[1:2, :], b_ref[1:2, :])

    x, proj = project()
    if pipelined:
        mix = recurrence_and_mix(jnp.maximum(s - 1, 0), s > 0)
        conv_and_gates(proj)
        residual_norm(mix)
        x_scr[...] = x
    else:
        x_scr[...] = x
        conv_and_gates(proj)
        residual_norm(recurrence_and_mix(s, None))

    @pl.when(s == n_chunks - 1)
    def _():
        for k in range(CONV_W - 1):
            j = HIST - (CONV_W - 1) + k
            cnew_ref[:, k, :] = ul_ext[j * nb:(j + 1) * nb, :]

    @pl.when(s == pl.num_programs(0) - 1)
    def _():
        for k in range(POOL_MAX_WIN - 1):
            j = HIST - (POOL_MAX_WIN - 1) + k
            pnew_ref[:, k, :] = up_ext[j * nb:(j + 1) * nb, :]


def _mixer(x, h0, cst, pst, win, cw, cb, gwt, gab, gxb, lam, pw, ps, wout, ln_g, ln_b,
           *, layer, state_layer, nb, start_pos):
    n_rows = x.shape[0]
    t_total = n_rows // nb
    tt = min(MIXER_TIME_TILE, t_total)
    assert n_rows % nb == 0 and t_total % tt == 0 and nb % V7X_SUBLANES == 0
    m = tt * nb
    n_chunks = t_total // tt
    n_steps = n_chunks + 1 if n_chunks > 1 else 1
    hr = HIST * nb
    gw = GROUP_WIDTH
    kern = functools.partial(_mixer_kernel, nb=nb, tt=tt, n_chunks=n_chunks, start_pos=start_pos,
                             layer=layer)
    stacked = lambda p: _layer_resident(p.shape, layer)
    out_blk = (lambda i: (jnp.maximum(i - 1, 0), 0)) if n_chunks > 1 else (lambda i: (0, 0))
    return pl.pallas_call(
        kern,
        grid=(n_steps,),
        in_specs=[pl.BlockSpec((m, D_MODEL), lambda i: (jnp.minimum(i, n_chunks - 1), 0)),
                  _layer_resident(h0.shape, state_layer), _layer_resident(cst.shape, state_layer),
                  _layer_resident(pst.shape, state_layer),
                  stacked(win), stacked(cw), _resident(cb.shape), stacked(gwt),
                  _resident(gab.shape), _resident(gxb.shape), _resident(lam.shape),
                  stacked(pw), _resident(ps.shape), stacked(wout), stacked(ln_g), stacked(ln_b)],
        out_specs=[pl.BlockSpec((m, D_MODEL), out_blk),
                   pl.BlockSpec((nb, gw), lambda i: (0, 0)),
                   pl.BlockSpec((nb, CONV_W - 1, gw), lambda i: (0, 0, 0)),
                   pl.BlockSpec((nb, POOL_MAX_WIN - 1, gw), lambda i: (0, 0, 0))],
        out_shape=[jax.ShapeDtypeStruct(x.shape, _F32),
                   jax.ShapeDtypeStruct((nb, gw), _F32),
                   jax.ShapeDtypeStruct((nb, CONV_W - 1, gw), _F32),
                   jax.ShapeDtypeStruct((nb, POOL_MAX_WIN - 1, gw), _F32)],
        scratch_shapes=[pltpu.VMEM((hr + m, gw), _F32),
                        pltpu.VMEM((hr + m, gw), _F32),
                        pltpu.VMEM((nb, gw), _F32),
                        pltpu.VMEM((m, gw), _F32),
                        pltpu.VMEM((m, gw), _F32),
                        pltpu.VMEM((m, gw), _F32),
                        pltpu.VMEM((m, gw), _F32),
                        pltpu.VMEM((m, gw), _F32),
                        pltpu.VMEM((m, D_MODEL), _F32)],
        compiler_params=pltpu.CompilerParams(
            dimension_semantics=("arbitrary",),
            vmem_limit_bytes=V7X_VMEM_LIMIT_BYTES),
        name="mixer",
    )(x, h0, cst, pst, win, cw, cb, gwt, gab, gxb, lam, pw, ps, wout, ln_g, ln_b)


def _block_diag_halves(w):
    groups, n = w.shape[1], w.shape[2]
    gh = groups // 2
    halves = []
    for p in range(2):
        rows = [jnp.pad(w[:, p * gh + g], ((0, 0), (0, 0), (g * n, (gh - 1 - g) * n)))
                for g in range(gh)]
        halves.append(jnp.concatenate(rows, axis=1))
    return jnp.stack(halves, axis=1)


def kernel(x_prompt, x_sample, state_lru_h, state_conv, state_pool, ln_g, ln_b,
           w1_gate, w1_up, w1_down, w_in, conv_w, conv_b, gate_a_w, gate_a_b,
           gate_x_w, gate_x_b, lru_lambda, pool_w, pool_scale, w_out,
           w2_gate, w2_up, w2_down):
    batch, seq, d = x_prompt.shape
    dec_batch = x_sample.shape[0]
    gw = GROUP_WIDTH

    win = w_in.astype(_BF16)
    wout = w_out.astype(_BF16)
    gate_w = jnp.concatenate(
        [_block_diag_halves(gate_a_w), _block_diag_halves(gate_x_w)], axis=-1).astype(_BF16)
    pool_wb = _block_diag_halves(pool_w).astype(_BF16)
    mix_params = (win, conv_w, conv_b, gate_w, gate_a_b, gate_x_b, lru_lambda, pool_wb,
                  pool_scale, wout, ln_g, ln_b)

    xp = x_prompt
    xs = x_sample.reshape(dec_batch, d)
    zeros_h = jnp.zeros((1, batch, gw), _F32)
    zeros_conv = jnp.zeros((1, batch, CONV_W - 1, gw), _F32)
    zeros_pool = jnp.zeros((1, batch, POOL_MAX_WIN - 1, gw), _F32)

    p_h, p_conv, p_pool, s_h, s_conv, s_pool = [], [], [], [], [], []
    for l in range(DEPTH):
        xp, xs = _ffn(xp, xs, w1_gate, w1_up, w1_down, ln_g, ln_b, layer=l, ln_row=0,
                      layout="batch_to_time" if l == 0 else "rows")
        xp = xp.reshape(seq * batch, d)
        xp, h_t, cnew, pnew = _mixer(xp, zeros_h, zeros_conv, zeros_pool, *mix_params,
                                     layer=l, state_layer=0, nb=batch, start_pos=0)
        p_h.append(h_t); p_conv.append(cnew); p_pool.append(pnew)
        xs, h_t, cnew, pnew = _mixer(xs, state_lru_h, state_conv, state_pool, *mix_params,
                                     layer=l, state_layer=l, nb=dec_batch, start_pos=PAST_LEN)
        s_h.append(h_t); s_conv.append(cnew); s_pool.append(pnew)
        if l == DEPTH - 1:
            xp = xp.reshape(seq, batch, d)
        xp, xs = _ffn(xp, xs, w2_gate, w2_up, w2_down, ln_g, ln_b, layer=l, ln_row=2,
                      layout="time_to_batch" if l == DEPTH - 1 else "rows")

    y_sample = xs.reshape(dec_batch, 1, d)
    return (xp, y_sample,
            jnp.stack(p_h), jnp.stack(p_conv), jnp.stack(p_pool),
            jnp.stack(s_h), jnp.stack(s_conv), jnp.stack(s_pool))
```

```python
import functools
import math

import jax
import jax.numpy as jnp
from jax import lax
from jax.experimental import pallas as pl
from jax.experimental.pallas import tpu as pltpu

D_MODEL = 1024
DEPTH = 4
GROUP_WIDTH = 512
LRU_C = 8.0
CONV_W = 4
POOL_WINDOWS = (2, 4, 8, 16)
POOL_MAX_WIN = 16
D_FF = 2816
LN_EPS = 1e-5
DEEPNORM_ALPHA = (2.0 * DEPTH) ** 0.25
PAST_LEN = 16384
assert math.frexp(LRU_C)[0] == 0.5 and all(math.frexp(w)[0] == 0.5 for w in POOL_WINDOWS)

V7X_LANES = 128
V7X_SUBLANES = 8
V7X_MXU_DIM = 256
V7X_VMEM_LIMIT_BYTES = 56 * 1024 * 1024

HIST = POOL_MAX_WIN
FFN_TOKEN_TILE = 512
MIXER_TIME_TILE = 64
FFN_CHUNKS = (768, 768, 768, 512)
assert sum(FFN_CHUNKS) == D_FF and all(c % V7X_MXU_DIM == 0 for c in FFN_CHUNKS)
FFN_LOAD_SLOTS = 2

_BF16 = jnp.bfloat16
_F32 = jnp.float32


def _dot(a, b):
    return jnp.dot(a, b, preferred_element_type=_F32)


def _layernorm(y, g, b):
    mu = jnp.mean(y, axis=-1, keepdims=True)
    d = y - mu
    var = jnp.mean(d * d, axis=-1, keepdims=True)
    return d * lax.rsqrt(var + LN_EPS) * g + b


def _zero_tile_from(v, shape):
    u = lax.bitcast_convert_type(v, jnp.uint32)
    rows, cols = u.shape
    bf16_rows = 2 * V7X_SUBLANES
    acc = u[0:bf16_rows, :]
    for k in range(1, rows // bf16_rows):
        acc = acc | u[k * bf16_rows:(k + 1) * bf16_rows, :]
    word = acc[:, 0:V7X_LANES]
    for k in range(1, cols // V7X_LANES):
        word = word | acc[:, k * V7X_LANES:(k + 1) * V7X_LANES]
    zero = lax.bitcast_convert_type((word >> 16) >> 16, _F32).astype(_BF16)
    zero = jnp.concatenate([zero] * (shape[1] // V7X_LANES), axis=1)
    return jnp.concatenate([zero] * (shape[0] // bf16_rows), axis=0)


def _resident(shape):
    nd = len(shape)
    return pl.BlockSpec(shape, lambda *_: (0,) * nd, pipeline_mode=pl.Buffered(1))


def _layer_resident(shape, layer):
    nd = len(shape)
    return pl.BlockSpec((None,) + tuple(shape[1:]), lambda *_: (layer,) + (0,) * (nd - 1),
                        pipeline_mode=pl.Buffered(1))


def _ffn_weight_loader(layer, wg_hbm, wu_hbm, wd_hbm, wg_v, wu_v, wd_v, st_gu, st_d, sems):
    offsets = [sum(FFN_CHUNKS[:p]) for p in range(len(FFN_CHUNKS))]

    def copies(p):
        slot = p % FFN_LOAD_SLOTS
        cols, head = pl.ds(offsets[p], FFN_CHUNKS[p]), pl.ds(0, FFN_CHUNKS[p])
        return (pltpu.make_async_copy(wg_hbm.at[layer, :, cols], st_gu.at[slot, 0, :, head], sems.at[slot, 0]),
                pltpu.make_async_copy(wu_hbm.at[layer, :, cols], st_gu.at[slot, 1, :, head], sems.at[slot, 1]),
                pltpu.make_async_copy(wd_hbm.at[layer, cols, :], st_d.at[slot, head, :], sems.at[slot, 2]))

    def start(p):
        if p < len(FFN_CHUNKS):
            for cp in copies(p):
                cp.start()

    def finish(p):
        slot = p % FFN_LOAD_SLOTS
        for cp in copies(p):
            cp.wait()
        lo, w = offsets[p], FFN_CHUNKS[p]
        wg_v[:, lo:lo + w] = st_gu[slot, 0, :, 0:w].astype(_BF16)
        wu_v[:, lo:lo + w] = st_gu[slot, 1, :, 0:w].astype(_BF16)
        wd_v[lo:lo + w, :] = st_d[slot, 0:w, :].astype(_BF16)

    return start, finish


def _swiglu(xb, wg_v, wu_v, wd_v, per_chunk=None, before_chunk=None):
    acc = jnp.zeros((xb.shape[0], D_MODEL), _F32)
    lo = 0
    for ci, width in enumerate(FFN_CHUNKS):
        if before_chunk is not None:
            before_chunk(ci)
        gate = _dot(xb, wg_v[:, lo:lo + width])
        up = _dot(xb, wu_v[:, lo:lo + width])
        hid = (jax.nn.silu(gate) * up).astype(_BF16)
        if per_chunk is not None:
            hid = per_chunk(ci, hid)
        acc = acc + _dot(hid, wd_v[lo:lo + width, :])
        lo += width
    return acc


def _ffn_kernel(xp_ref, xs_ref, wg_hbm, wu_hbm, wd_hbm, g_ref, b_ref, op_ref, os_ref,
                wg_v, wu_v, wd_v, st_gu, st_d, sems, y_scr, ys_scr,
                *, layer, ln_row, n_tiles, layout):
    s = pl.program_id(0)
    ln = lambda y: _layernorm(y, g_ref[ln_row:ln_row + 1, :], b_ref[ln_row:ln_row + 1, :])

    if layout == "rows":
        load_tile = lambda: xp_ref[...]
        def store_rows(r0, r1, val):
            op_ref[r0:r1, :] = val
    elif layout == "batch_to_time":
        nbat, tt = xp_ref.shape[0], xp_ref.shape[1]
        load_tile = lambda: xp_ref[...].reshape(nbat * tt, D_MODEL)
        def store_rows(r0, r1, val):
            for b in range(r0 // tt, r1 // tt):
                op_ref[:, b, :] = val[b * tt - r0:(b + 1) * tt - r0, :]
    else:
        tt, nbat = xp_ref.shape[0], xp_ref.shape[1]
        load_tile = lambda: jnp.concatenate([xp_ref[:, b, :] for b in range(nbat)], axis=0)
        def store_rows(r0, r1, val):
            op_ref[r0 // tt:r1 // tt, :, :] = val.reshape((r1 - r0) // tt, tt, D_MODEL)

    def prompt_tile(before_chunk=None):
        x = load_tile()
        part = x.shape[0] // len(FFN_CHUNKS)

        def ln_part(ci, hid):
            rows = slice(ci * part, (ci + 1) * part)
            normed = ln(y_scr[rows, :])
            store_rows(ci * part, (ci + 1) * part, normed)
            return hid + _zero_tile_from(normed, hid.shape)

        acc = _swiglu(x.astype(_BF16), wg_v, wu_v, wd_v, ln_part, before_chunk)
        y_scr[...] = DEEPNORM_ALPHA * x + 0.5 * acc

    @pl.when(s == 0)
    def _():
        y_scr[...] = jnp.zeros(y_scr.shape, _F32)
        start, finish = _ffn_weight_loader(layer, wg_hbm, wu_hbm, wd_hbm, wg_v, wu_v, wd_v,
                                           st_gu, st_d, sems)
        for p in range(FFN_LOAD_SLOTS):
            start(p)

        def load_chunk(ci):
            finish(ci)
            start(ci + FFN_LOAD_SLOTS)

        prompt_tile(load_chunk)

    @pl.when((s > 0) & (s < n_tiles))
    def _():
        prompt_tile()

    @pl.when(s == n_tiles)
    def _():
        store_rows(0, y_scr.shape[0], ln(y_scr[...]))
        xs = xs_ref[...]
        ys_scr[...] = DEEPNORM_ALPHA * xs + 0.5 * _swiglu(xs.astype(_BF16), wg_v, wu_v, wd_v)

    @pl.when(s == n_tiles + 1)
    def _():
        os_ref[...] = ln(ys_scr[...])


def _ffn(xp, xs, wg, wu, wd, ln_g, ln_b, *, layer, ln_row, layout="rows"):
    ms = xs.shape[0]
    tm = FFN_TOKEN_TILE
    assert tm % (len(FFN_CHUNKS) * 2 * V7X_SUBLANES) == 0
    if layout == "rows":
        m = xp.shape[0]
        in_block, out_block, out_full = (tm, D_MODEL), (tm, D_MODEL), (m, D_MODEL)
        at = lambda j: (j, 0)
        in_at = out_at = at
    else:
        nbat, t_total = (xp.shape[0], xp.shape[1]) if layout == "batch_to_time" else (xp.shape[1], xp.shape[0])
        m = nbat * t_total
        tt = tm // nbat
        part_rows = tm // len(FFN_CHUNKS)
        assert tm % nbat == 0 and tt % V7X_SUBLANES == 0 and part_rows % tt == 0
        batch_major = ((nbat, tt, D_MODEL), lambda j: (0, j, 0), (nbat, t_total, D_MODEL))
        time_major = ((tt, nbat, D_MODEL), lambda j: (j, 0, 0), (t_total, nbat, D_MODEL))
        src, dst = (batch_major, time_major) if layout == "batch_to_time" else (time_major, batch_major)
        in_block, in_at = src[0], src[1]
        out_block, out_at, out_full = dst
    assert m % tm == 0
    n_tiles = m // tm
    hbm = pl.BlockSpec(memory_space=pl.ANY)
    return pl.pallas_call(
        functools.partial(_ffn_kernel, layer=layer, ln_row=ln_row, n_tiles=n_tiles, layout=layout),
        grid=(n_tiles + 2,),
        in_specs=[
            pl.BlockSpec(in_block, lambda i: in_at(jnp.minimum(i, n_tiles - 1))),
            pl.BlockSpec((ms, D_MODEL), lambda i: (0, 0)),
            hbm, hbm, hbm,
            _layer_resident(ln_g.shape, layer), _layer_resident(ln_b.shape, layer),
        ],
        out_specs=[pl.BlockSpec(out_block, lambda i: out_at(jnp.clip(i - 1, 0, n_tiles - 1))),
                   pl.BlockSpec((ms, D_MODEL), lambda i: (0, 0))],
        out_shape=[jax.ShapeDtypeStruct(out_full, _F32),
                   jax.ShapeDtypeStruct((ms, D_MODEL), _F32)],
        scratch_shapes=[pltpu.VMEM((D_MODEL, D_FF), _BF16),
                        pltpu.VMEM((D_MODEL, D_FF), _BF16),
                        pltpu.VMEM((D_FF, D_MODEL), _BF16),
                        pltpu.VMEM((FFN_LOAD_SLOTS, 2, D_MODEL, max(FFN_CHUNKS)), _F32),
                        pltpu.VMEM((FFN_LOAD_SLOTS, max(FFN_CHUNKS), D_MODEL), _F32),
                        pltpu.SemaphoreType.DMA((FFN_LOAD_SLOTS, 3)),
                        pltpu.VMEM((tm, D_MODEL), _F32),
                        pltpu.VMEM((ms, D_MODEL), _F32)],
        compiler_params=pltpu.CompilerParams(
            dimension_semantics=("arbitrary",),
            vmem_limit_bytes=V7X_VMEM_LIMIT_BYTES),
        name="ffn",
    )(xp, xs, wg, wu, wd, ln_g, ln_b)


def _mixer_kernel(x_ref, h0_ref, cst_ref, pst_ref, win_ref, cw_ref, cb_ref, gw_ref, gab_ref,
                  gxb_ref, lam_ref, pw_ref, ps_ref, wout_ref, g_ref, b_ref,
                  o_ref, ht_ref, cnew_ref, pnew_ref,
                  ul_ext, up_ext, h_scr, ga_scr, gx_scr, xc_scr, hs_scr, gate_scr, x_scr,
                  *, nb, tt, n_chunks, start_pos, layer):
    s = pl.program_id(0)
    pipelined = n_chunks > 1
    m = tt * nb
    hr = HIST * nb
    gw = GROUP_WIDTH
    half = gw // 2
    vec = lambda ref: ref[layer:layer + 1, :]

    @pl.when(s == 0)
    def _():
        h_scr[...] = h0_ref[...]
        ul_ext[0:hr, :] = jnp.zeros((hr, gw), _F32)
        for k in range(CONV_W - 1):
            j = HIST - (CONV_W - 1) + k
            ul_ext[j * nb:(j + 1) * nb, :] = cst_ref[:, k, :]
        if pipelined:
            for ref in (ga_scr, gx_scr, xc_scr, gate_scr, x_scr):
                ref[...] = jnp.zeros(ref.shape, _F32)
            up_ext[0:m, :] = jnp.zeros((m, gw), _F32)
        pool0 = m if pipelined else 0
        up_ext[pool0:pool0 + nb, :] = jnp.zeros((nb, gw), _F32)
        for k in range(POOL_MAX_WIN - 1):
            j = HIST - (POOL_MAX_WIN - 1) + k
            up_ext[pool0 + j * nb:pool0 + (j + 1) * nb, :] = pst_ref[:, k, :]

    def project():
        x = x_ref[...]
        return x, _dot(x.astype(_BF16), win_ref[...])

    def conv_and_gates(proj):
        u_lru = proj[:, 0:gw]
        gate_scr[...] = proj[:, gw:2 * gw]
        ul_ext[hr:hr + m, :] = u_lru
        up_ext[hr:hr + m, :] = proj[:, 2 * gw:3 * gw]
        xc = vec(cb_ref) + cw_ref[CONV_W - 1:CONV_W, :] * u_lru
        for k in range(CONV_W - 1):
            off = hr - (CONV_W - 1 - k) * nb
            xc = xc + cw_ref[k:k + 1, :] * ul_ext[off:off + m, :]
        xc_scr[...] = xc
        xcb = xc.astype(_BF16)
        for p in range(2):
            cols = slice(p * half, (p + 1) * half)
            gts = _dot(xcb[:, cols], gw_ref[p])
            ga_scr[:, cols] = gts[:, 0:half]
            gx_scr[:, cols] = gts[:, half:2 * half]
        ul_ext[0:hr, :] = ul_ext[m:m + hr, :]

    def recurrence_and_mix(chunk, live):
        xc = xc_scr[...]
        r = jax.nn.sigmoid(ga_scr[...] + vec(gab_ref))
        i = jax.nn.sigmoid(gx_scr[...] + vec(gxb_ref))
        nl = -vec(lam_ref)
        softplus_nl = jnp.maximum(nl, 0.0) + jnp.log1p(jnp.exp(-jnp.abs(nl)))
        log_a = r * (-LRU_C * softplus_nl)
        a = jnp.exp(log_a)
        one_m_a2 = 1.0 - a * a
        mult = jnp.where(one_m_a2 > 0.0, one_m_a2 * lax.rsqrt(one_m_a2), one_m_a2)
        uu = mult * (i * xc)
        h_in = h_scr[...]
        h = h_in
        for t in range(tt):
            rows = slice(t * nb, (t + 1) * nb)
            h = a[rows, :] * h + uu[rows, :]
            hs_scr[rows, :] = h
        if live is not None:
            h = jnp.where(live, h, h_in)
        h_scr[...] = h
        ht_ref[...] = h
        y_lru = hs_scr[...] * jax.nn.gelu(gate_scr[...])

        t_loc = lax.broadcasted_iota(jnp.int32, (tt, nb, V7X_LANES), 0).reshape(m, V7X_LANES)
        pos = start_pos + chunk * tt + t_loc
        pooled = []
        for gi, w in enumerate(POOL_WINDOWS):
            lanes = slice(gi * V7X_LANES, (gi + 1) * V7X_LANES)
            n_rows = (w - 1) * nb + m
            sums = up_ext[hr - (w - 1) * nb:hr + m, lanes]
            span = 1
            while span < w:
                n_rows -= span * nb
                sums = sums[span * nb:span * nb + n_rows, :] + sums[0:n_rows, :]
                span *= 2
            head = min(m, (w - 1) * nb)
            cnt = jnp.minimum(w, pos[0:head, :] + 1).astype(_F32)
            mean = sums[0:head, :] / cnt
            if head < m:
                mean = jnp.concatenate([mean, sums[head:m, :] * (1.0 / w)], axis=0)
            pooled.append(mean - up_ext[hr:hr + m, lanes])
        pooled = jnp.concatenate(pooled, axis=-1).astype(_BF16)
        y_pool = jnp.concatenate(
            [_dot(pooled[:, p * half:(p + 1) * half], pw_ref[p]) for p in range(2)],
            axis=-1) * vec(ps_ref)

        mix = _dot(jnp.concatenate([y_lru, y_pool], axis=-1).astype(_BF16), wout_ref[...])
        up_ext[0:hr, :] = up_ext[m:m + hr, :]
        return mix

    def residual_norm(mix):
        y = DEEPNORM_ALPHA * x_scr[...] + mix
        o_ref[...] = _layernorm(y, g_ref[1:2, :], b_ref[1:2, :])

    x, proj = project()
    if pipelined:
        mix = recurrence_and_mix(jnp.maximum(s - 1, 0), s > 0)
        conv_and_gates(proj)
        residual_norm(mix)
        x_scr[...] = x
    else:
        x_scr[...] = x
        conv_and_gates(proj)
        residual_norm(recurrence_and_mix(s, None))

    @pl.when(s == n_chunks - 1)
    def _():
        for k in range(CONV_W - 1):
            j = HIST - (CONV_W - 1) + k
            cnew_ref[:, k, :] = ul_ext[j * nb:(j + 1) * nb, :]

    @pl.when(s == pl.num_programs(0) - 1)
    def _():
        for k in range(POOL_MAX_WIN - 1):
            j = HIST - (POOL_MAX_WIN - 1) + k
            pnew_ref[:, k, :] = up_ext[j * nb:(j + 1) * nb, :]


def _mixer(x, h0, cst, pst, win, cw, cb, gwt, gab, gxb, lam, pw, ps, wout, ln_g, ln_b,
           *, layer, state_layer, nb, start_pos):
    n_rows = x.shape[0]
    t_total = n_rows // nb
    tt = min(MIXER_TIME_TILE, t_total)
    assert n_rows % nb == 0 and t_total % tt == 0 and nb % V7X_SUBLANES == 0
    m = tt * nb
    n_chunks = t_total // tt
    n_steps = n_chunks + 1 if n_chunks > 1 else 1
    hr = HIST * nb
    gw = GROUP_WIDTH
    kern = functools.partial(_mixer_kernel, nb=nb, tt=tt, n_chunks=n_chunks, start_pos=start_pos,
                             layer=layer)
    stacked = lambda p: _layer_resident(p.shape, layer)
    out_blk = (lambda i: (jnp.maximum(i - 1, 0), 0)) if n_chunks > 1 else (lambda i: (0, 0))
    return pl.pallas_call(
        kern,
        grid=(n_steps,),
        in_specs=[pl.BlockSpec((m, D_MODEL), lambda i: (jnp.minimum(i, n_chunks - 1), 0)),
                  _layer_resident(h0.shape, state_layer), _layer_resident(cst.shape, state_layer),
                  _layer_resident(pst.shape, state_layer),
                  stacked(win), stacked(cw), _resident(cb.shape), stacked(gwt),
                  _resident(gab.shape), _resident(gxb.shape), _resident(lam.shape),
                  stacked(pw), _resident(ps.shape), stacked(wout), stacked(ln_g), stacked(ln_b)],
        out_specs=[pl.BlockSpec((m, D_MODEL), out_blk),
                   pl.BlockSpec((nb, gw), lambda i: (0, 0)),
                   pl.BlockSpec((nb, CONV_W - 1, gw), lambda i: (0, 0, 0)),
                   pl.BlockSpec((nb, POOL_MAX_WIN - 1, gw), lambda i: (0, 0, 0))],
        out_shape=[jax.ShapeDtypeStruct(x.shape, _F32),
                   jax.ShapeDtypeStruct((nb, gw), _F32),
                   jax.ShapeDtypeStruct((nb, CONV_W - 1, gw), _F32),
                   jax.ShapeDtypeStruct((nb, POOL_MAX_WIN - 1, gw), _F32)],
        scratch_shapes=[pltpu.VMEM((hr + m, gw), _F32),
                        pltpu.VMEM((hr + m, gw), _F32),
                        pltpu.VMEM((nb, gw), _F32),
                        pltpu.VMEM((m, gw), _F32),
                        pltpu.VMEM((m, gw), _F32),
                        pltpu.VMEM((m, gw), _F32),
                        pltpu.VMEM((m, gw), _F32),
                        pltpu.VMEM((m, gw), _F32),
                        pltpu.VMEM((m, D_MODEL), _F32)],
        compiler_params=pltpu.CompilerParams(
            dimension_semantics=("arbitrary",),
            vmem_limit_bytes=V7X_VMEM_LIMIT_BYTES),
        name="mixer",
    )(x, h0, cst, pst, win, cw, cb, gwt, gab, gxb, lam, pw, ps, wout, ln_g, ln_b)


def _block_diag_halves(w):
    groups, n = w.shape[1], w.shape[2]
    gh = groups // 2
    halves = []
    for p in range(2):
        rows = [jnp.pad(w[:, p * gh + g], ((0, 0), (0, 0), (g * n, (gh - 1 - g) * n)))
                for g in range(gh)]
        halves.append(jnp.concatenate(rows, axis=1))
    return jnp.stack(halves, axis=1)


def kernel(x_prompt, x_sample, state_lru_h, state_conv, state_pool, ln_g, ln_b,
           w1_gate, w1_up, w1_down, w_in, conv_w, conv_b, gate_a_w, gate_a_b,
           gate_x_w, gate_x_b, lru_lambda, pool_w, pool_scale, w_out,
           w2_gate, w2_up, w2_down):
    batch, seq, d = x_prompt.shape
    dec_batch = x_sample.shape[0]
    gw = GROUP_WIDTH

    win = w_in.astype(_BF16)
    wout = w_out.astype(_BF16)
    gate_w = jnp.concatenate(
        [_block_diag_halves(gate_a_w), _block_diag_halves(gate_x_w)], axis=-1).astype(_BF16)
    pool_wb = _block_diag_halves(pool_w).astype(_BF16)
    mix_params = (win, conv_w, conv_b, gate_w, gate_a_b, gate_x_b, lru_lambda, pool_wb,
                  pool_scale, wout, ln_g, ln_b)

    xp = x_prompt
    xs = x_sample.reshape(dec_batch, d)
    zeros_h = jnp.zeros((1, batch, gw), _F32)
    zeros_conv = jnp.zeros((1, batch, CONV_W - 1, gw), _F32)
    zeros_pool = jnp.zeros((1, batch, POOL_MAX_WIN - 1, gw), _F32)

    p_h, p_conv, p_pool, s_h, s_conv, s_pool = [], [], [], [], [], []
    for l in range(DEPTH):
        xp, xs = _ffn(xp, xs, w1_gate, w1_up, w1_down, ln_g, ln_b, layer=l, ln_row=0,
                      layout="batch_to_time" if l == 0 else "rows")
        xp = xp.reshape(seq * batch, d)
        xp, h_t, cnew, pnew = _mixer(xp, zeros_h, zeros_conv, zeros_pool, *mix_params,
                                     layer=l, state_layer=0, nb=batch, start_pos=0)
        p_h.append(h_t); p_conv.append(cnew); p_pool.append(pnew)
        xs, h_t, cnew, pnew = _mixer(xs, state_lru_h, state_conv, state_pool, *mix_params,
                                     layer=l, state_layer=l, nb=dec_batch, start_pos=PAST_LEN)
        s_h.append(h_t); s_conv.append(cnew); s_pool.append(pnew)
        if l == DEPTH - 1:
            xp = xp.reshape(seq, batch, d)
        xp, xs = _ffn(xp, xs, w2_gate, w2_up, w2_down, ln_g, ln_b, layer=l, ln_row=2,
                      layout="time_to_batch" if l == DEPTH - 1 else "rows")

    y_sample = xs.reshape(dec_batch, 1, d)
    return (xp, y_sample,
            jnp.stack(p_h), jnp.stack(p_conv), jnp.stack(p_pool),
            jnp.stack(s_h), jnp.stack(s_conv), jnp.stack(s_pool))
```

```python
import functools
import math

import jax
import jax.numpy as jnp
from jax import lax
from jax.experimental import pallas as pl
from jax.experimental.pallas import tpu as pltpu

D_MODEL = 1024
DEPTH = 4
GROUP_WIDTH = 512
LRU_C = 8.0
CONV_W = 4
POOL_WINDOWS = (2, 4, 8, 16)
POOL_MAX_WIN = 16
D_FF = 2816
LN_EPS = 1e-5
DEEPNORM_ALPHA = (2.0 * DEPTH) ** 0.25
PAST_LEN = 16384
assert math.frexp(LRU_C)[0] == 0.5 and all(math.frexp(w)[0] == 0.5 for w in POOL_WINDOWS)

V7X_LANES = 128
V7X_SUBLANES = 8
V7X_MXU_DIM = 256
V7X_VMEM_LIMIT_BYTES = 56 * 1024 * 1024

HIST = POOL_MAX_WIN
FFN_TOKEN_TILE = 512
MIXER_TIME_TILE = 64
FFN_CHUNKS = (768, 768, 768, 512)
assert sum(FFN_CHUNKS) == D_FF and all(c % V7X_MXU_DIM == 0 for c in FFN_CHUNKS)
FFN_LOAD_SLOTS = 2

_BF16 = jnp.bfloat16
_F32 = jnp.float32


def _dot(a, b):
    return jnp.dot(a, b, preferred_element_type=_F32)


def _layernorm(y, g, b):
    mu = jnp.mean(y, axis=-1, keepdims=True)
    d = y - mu
    var = jnp.mean(d * d, axis=-1, keepdims=True)
    return d * lax.rsqrt(var + LN_EPS) * g + b


def _zero_tile_from(v, shape):
    u = lax.bitcast_convert_type(v, jnp.uint32)
    rows, cols = u.shape
    bf16_rows = 2 * V7X_SUBLANES
    acc = u[0:bf16_rows, :]
    for k in range(1, rows // bf16_rows):
        acc = acc | u[k * bf16_rows:(k + 1) * bf16_rows, :]
    word = acc[:, 0:V7X_LANES]
    for k in range(1, cols // V7X_LANES):
        word = word | acc[:, k * V7X_LANES:(k + 1) * V7X_LANES]
    zero = lax.bitcast_convert_type((word >> 16) >> 16, _F32).astype(_BF16)
    zero = jnp.concatenate([zero] * (shape[1] // V7X_LANES), axis=1)
    return jnp.concatenate([zero] * (shape[0] // bf16_rows), axis=0)


def _resident(shape):
    nd = len(shape)
    return pl.BlockSpec(shape, lambda *_: (0,) * nd, pipeline_mode=pl.Buffered(1))


def _layer_resident(shape, layer):
    nd = len(shape)
    return pl.BlockSpec((None,) + tuple(shape[1:]), lambda *_: (layer,) + (0,) * (nd - 1),
                        pipeline_mode=pl.Buffered(1))


def _ffn_weight_loader(layer, wg_hbm, wu_hbm, wd_hbm, wg_v, wu_v, wd_v, st_gu, st_d, sems):
    offsets = [sum(FFN_CHUNKS[:p]) for p in range(len(FFN_CHUNKS))]

    def copies(p):
        slot = p % FFN_LOAD_SLOTS
        cols, head = pl.ds(offsets[p], FFN_CHUNKS[p]), pl.ds(0, FFN_CHUNKS[p])
        top, bottom = pl.ds(0, D_MODEL // 2), pl.ds(D_MODEL // 2, D_MODEL // 2)
        halves = [(wg_hbm, 0, top), (wu_hbm, 1, bottom), (wg_hbm, 0, bottom), (wu_hbm, 1, top)]
        cps = [pltpu.make_async_copy(w.at[layer, rows, cols], st_gu.at[slot, j, rows, head],
                                     sems.at[slot, k])
               for k, (w, j, rows) in enumerate(halves)]
        cps.append(pltpu.make_async_copy(wd_hbm.at[layer, cols, :], st_d.at[slot, head, :],
                                         sems.at[slot, len(halves)]))
        return cps

    def start(p):
        if p < len(FFN_CHUNKS):
            for cp in copies(p):
                cp.start()

    def finish(p):
        slot = p % FFN_LOAD_SLOTS
        for cp in copies(p):
            cp.wait()
        lo, w = offsets[p], FFN_CHUNKS[p]
        wg_v[:, lo:lo + w] = st_gu[slot, 0, :, 0:w].astype(_BF16)
        wu_v[:, lo:lo + w] = st_gu[slot, 1, :, 0:w].astype(_BF16)
        wd_v[lo:lo + w, :] = st_d[slot, 0:w, :].astype(_BF16)

    return start, finish


def _swiglu(xb, wg_v, wu_v, wd_v, per_chunk=None, before_chunk=None):
    acc = jnp.zeros((xb.shape[0], D_MODEL), _F32)
    lo = 0
    for ci, width in enumerate(FFN_CHUNKS):
        if before_chunk is not None:
            before_chunk(ci)
        gate = _dot(xb, wg_v[:, lo:lo + width])
        up = _dot(xb, wu_v[:, lo:lo + width])
        hid = (jax.nn.silu(gate) * up).astype(_BF16)
        if per_chunk is not None:
            hid = per_chunk(ci, hid)
        acc = acc + _dot(hid, wd_v[lo:lo + width, :])
        lo += width
    return acc


def _ffn_kernel(xp_ref, xs_ref, wg_hbm, wu_hbm, wd_hbm, g_ref, b_ref, op_ref, os_ref,
                wg_v, wu_v, wd_v, st_gu, st_d, sems, y_scr, ys_scr,
                *, layer, ln_row, n_tiles, layout):
    s = pl.program_id(0)
    ln = lambda y: _layernorm(y, g_ref[ln_row:ln_row + 1, :], b_ref[ln_row:ln_row + 1, :])

    if layout == "rows":
        load_tile = lambda: xp_ref[...]
        def store_rows(r0, r1, val):
            op_ref[r0:r1, :] = val
    elif layout == "batch_to_time":
        nbat, tt = xp_ref.shape[0], xp_ref.shape[1]
        load_tile = lambda: xp_ref[...].reshape(nbat * tt, D_MODEL)
        def store_rows(r0, r1, val):
            for b in range(r0 // tt, r1 // tt):
                op_ref[:, b, :] = val[b * tt - r0:(b + 1) * tt - r0, :]
    else:
        tt, nbat = xp_ref.shape[0], xp_ref.shape[1]
        load_tile = lambda: jnp.concatenate([xp_ref[:, b, :] for b in range(nbat)], axis=0)
        def store_rows(r0, r1, val):
            op_ref[r0 // tt:r1 // tt, :, :] = val.reshape((r1 - r0) // tt, tt, D_MODEL)

    def prompt_tile(before_chunk=None):
        x = load_tile()
        part = x.shape[0] // len(FFN_CHUNKS)

        def ln_part(ci, hid):
            rows = slice(ci * part, (ci + 1) * part)
            normed = ln(y_scr[rows, :])
            store_rows(ci * part, (ci + 1) * part, normed)
            return hid + _zero_tile_from(normed, hid.shape)

        acc = _swiglu(x.astype(_BF16), wg_v, wu_v, wd_v, ln_part, before_chunk)
        y_scr[...] = DEEPNORM_ALPHA * x + 0.5 * acc

    @pl.when(s == 0)
    def _():
        y_scr[...] = jnp.zeros(y_scr.shape, _F32)
        start, finish = _ffn_weight_loader(layer, wg_hbm, wu_hbm, wd_hbm, wg_v, wu_v, wd_v,
                                           st_gu, st_d, sems)
        for p in range(FFN_LOAD_SLOTS):
            start(p)

        def load_chunk(ci):
            finish(ci)
            start(ci + FFN_LOAD_SLOTS)

        prompt_tile(load_chunk)

    @pl.when((s > 0) & (s < n_tiles))
    def _():
        prompt_tile()

    @pl.when(s == n_tiles)
    def _():
        store_rows(0, y_scr.shape[0], ln(y_scr[...]))
        xs = xs_ref[...]
        ys_scr[...] = DEEPNORM_ALPHA * xs + 0.5 * _swiglu(xs.astype(_BF16), wg_v, wu_v, wd_v)

    @pl.when(s == n_tiles + 1)
    def _():
        os_ref[...] = ln(ys_scr[...])


def _ffn(xp, xs, wg, wu, wd, ln_g, ln_b, *, layer, ln_row, layout="rows"):
    ms = xs.shape[0]
    tm = FFN_TOKEN_TILE
    assert tm % (len(FFN_CHUNKS) * 2 * V7X_SUBLANES) == 0
    if layout == "rows":
        m = xp.shape[0]
        in_block, out_block, out_full = (tm, D_MODEL), (tm, D_MODEL), (m, D_MODEL)
        at = lambda j: (j, 0)
        in_at = out_at = at
    else:
        nbat, t_total = (xp.shape[0], xp.shape[1]) if layout == "batch_to_time" else (xp.shape[1], xp.shape[0])
        m = nbat * t_total
        tt = tm // nbat
        part_rows = tm // len(FFN_CHUNKS)
        assert tm % nbat == 0 and tt % V7X_SUBLANES == 0 and part_rows % tt == 0
        batch_major = ((nbat, tt, D_MODEL), lambda j: (0, j, 0), (nbat, t_total, D_MODEL))
        time_major = ((tt, nbat, D_MODEL), lambda j: (j, 0, 0), (t_total, nbat, D_MODEL))
        src, dst = (batch_major, time_major) if layout == "batch_to_time" else (time_major, batch_major)
        in_block, in_at = src[0], src[1]
        out_block, out_at, out_full = dst
    assert m % tm == 0
    n_tiles = m // tm
    hbm = pl.BlockSpec(memory_space=pl.ANY)
    return pl.pallas_call(
        functools.partial(_ffn_kernel, layer=layer, ln_row=ln_row, n_tiles=n_tiles, layout=layout),
        grid=(n_tiles + 2,),
        in_specs=[
            pl.BlockSpec(in_block, lambda i: in_at(jnp.minimum(i, n_tiles - 1))),
            pl.BlockSpec((ms, D_MODEL), lambda i: (0, 0)),
            hbm, hbm, hbm,
            _layer_resident(ln_g.shape, layer), _layer_resident(ln_b.shape, layer),
        ],
        out_specs=[pl.BlockSpec(out_block, lambda i: out_at(jnp.clip(i - 1, 0, n_tiles - 1))),
                   pl.BlockSpec((ms, D_MODEL), lambda i: (0, 0))],
        out_shape=[jax.ShapeDtypeStruct(out_full, _F32),
                   jax.ShapeDtypeStruct((ms, D_MODEL), _F32)],
        scratch_shapes=[pltpu.VMEM((D_MODEL, D_FF), _BF16),
                        pltpu.VMEM((D_MODEL, D_FF), _BF16),
                        pltpu.VMEM((D_FF, D_MODEL), _BF16),
                        pltpu.VMEM((FFN_LOAD_SLOTS, 2, D_MODEL, max(FFN_CHUNKS)), _F32),
                        pltpu.VMEM((FFN_LOAD_SLOTS, max(FFN_CHUNKS), D_MODEL), _F32),
                        pltpu.SemaphoreType.DMA((FFN_LOAD_SLOTS, 5)),
                        pltpu.VMEM((tm, D_MODEL), _F32),
                        pltpu.VMEM((ms, D_MODEL), _F32)],
        compiler_params=pltpu.CompilerParams(
            dimension_semantics=("arbitrary",),
            vmem_limit_bytes=V7X_VMEM_LIMIT_BYTES),
        name="ffn",
    )(xp, xs, wg, wu, wd, ln_g, ln_b)


def _mixer_kernel(x_ref, h0_ref, cst_ref, pst_ref, win_ref, cw_ref, cb_ref, gw_ref, gab_ref,
                  gxb_ref, lam_ref, pw_ref, ps_ref, wout_ref, g_ref, b_ref,
                  o_ref, ht_ref, cnew_ref, pnew_ref,
                  ul_ext, up_ext, h_scr, ga_scr, gx_scr, xc_scr, hs_scr, gate_scr, x_scr,
                  *, nb, tt, n_chunks, start_pos, layer):
    s = pl.program_id(0)
    pipelined = n_chunks > 1
    m = tt * nb
    hr = HIST * nb
    gw = GROUP_WIDTH
    half = gw // 2
    vec = lambda ref: ref[layer:layer + 1, :]

    @pl.when(s == 0)
    def _():
        h_scr[...] = h0_ref[...]
        ul_ext[0:hr, :] = jnp.zeros((hr, gw), _F32)
        for k in range(CONV_W - 1):
            j = HIST - (CONV_W - 1) + k
            ul_ext[j * nb:(j + 1) * nb, :] = cst_ref[:, k, :]
        if pipelined:
            for ref in (ga_scr, gx_scr, xc_scr, gate_scr, x_scr):
                ref[...] = jnp.zeros(ref.shape, _F32)
            up_ext[0:m, :] = jnp.zeros((m, gw), _F32)
        pool0 = m if pipelined else 0
        up_ext[pool0:pool0 + nb, :] = jnp.zeros((nb, gw), _F32)
        for k in range(POOL_MAX_WIN - 1):
            j = HIST - (POOL_MAX_WIN - 1) + k
            up_ext[pool0 + j * nb:pool0 + (j + 1) * nb, :] = pst_ref[:, k, :]

    def project():
        x = x_ref[...]
        return x, _dot(x.astype(_BF16), win_ref[...])

    def conv_and_gates(proj):
        u_lru = proj[:, 0:gw]
        gate_scr[...] = proj[:, gw:2 * gw]
        ul_ext[hr:hr + m, :] = u_lru
        up_ext[hr:hr + m, :] = proj[:, 2 * gw:3 * gw]
        xc = vec(cb_ref) + cw_ref[CONV_W - 1:CONV_W, :] * u_lru
        for k in range(CONV_W - 1):
            off = hr - (CONV_W - 1 - k) * nb
            xc = xc + cw_ref[k:k + 1, :] * ul_ext[off:off + m, :]
        xc_scr[...] = xc
        xcb = xc.astype(_BF16)
        for p in range(2):
            cols = slice(p * half, (p + 1) * half)
            gts = _dot(xcb[:, cols], gw_ref[p])
            ga_scr[:, cols] = gts[:, 0:half]
            gx_scr[:, cols] = gts[:, half:2 * half]
        ul_ext[0:hr, :] = ul_ext[m:m + hr, :]

    def recurrence_and_mix(chunk, live):
        xc = xc_scr[...]
        r = jax.nn.sigmoid(ga_scr[...] + vec(gab_ref))
        i = jax.nn.sigmoid(gx_scr[...] + vec(gxb_ref))
        nl = -vec(lam_ref)
        softplus_nl = jnp.maximum(nl, 0.0) + jnp.log1p(jnp.exp(-jnp.abs(nl)))
        log_a = r * (-LRU_C * softplus_nl)
        a = jnp.exp(log_a)
        one_m_a2 = 1.0 - a * a
        mult = jnp.where(one_m_a2 > 0.0, one_m_a2 * lax.rsqrt(one_m_a2), one_m_a2)
        uu = mult * (i * xc)
        h_in = h_scr[...]
        h = h_in
        for t in range(tt):
            rows = slice(t * nb, (t + 1) * nb)
            h = a[rows, :] * h + uu[rows, :]
            hs_scr[rows, :] = h
        if live is not None:
            h = jnp.where(live, h, h_in)
        h_scr[...] = h
        ht_ref[...] = h
        y_lru = hs_scr[...] * jax.nn.gelu(gate_scr[...])

        t_loc = lax.broadcasted_iota(jnp.int32, (tt, nb, V7X_LANES), 0).reshape(m, V7X_LANES)
        pos = start_pos + chunk * tt + t_loc
        pooled = []
        for gi, w in enumerate(POOL_WINDOWS):
            lanes = slice(gi * V7X_LANES, (gi + 1) * V7X_LANES)
            n_rows = (w - 1) * nb + m
            sums = up_ext[hr - (w - 1) * nb:hr + m, lanes]
            span = 1
            while span < w:
                n_rows -= span * nb
                sums = sums[span * nb:span * nb + n_rows, :] + sums[0:n_rows, :]
                span *= 2
            head = min(m, (w - 1) * nb)
            cnt = jnp.minimum(w, pos[0:head, :] + 1).astype(_F32)
            mean = sums[0:head, :] / cnt
            if head < m:
                mean = jnp.concatenate([mean, sums[head:m, :] * (1.0 / w)], axis=0)
            pooled.append(mean - up_ext[hr:hr + m, lanes])
        pooled = jnp.concatenate(pooled, axis=-1).astype(_BF16)
        y_pool = jnp.concatenate(
            [_dot(pooled[:, p * half:(p + 1) * half], pw_ref[p]) for p in range(2)],
            axis=-1) * vec(ps_ref)

        mix = _dot(jnp.concatenate([y_lru, y_pool], axis=-1).astype(_BF16), wout_ref[...])
        up_ext[0:hr, :] = up_ext[m:m + hr, :]
        return mix

    def residual_norm(mix):
        y = DEEPNORM_ALPHA * x_scr[...] + mix
        o_ref[...] = _layernorm(y, g_ref[1:2, :], b_ref[1:2, :])

    x, proj = project()
    if pipelined:
        mix = recurrence_and_mix(jnp.maximum(s - 1, 0), s > 0)
        conv_and_gates(proj)
        residual_norm(mix)
        x_scr[...] = x
    else:
        x_scr[...] = x
        conv_and_gates(proj)
        residual_norm(recurrence_and_mix(s, None))

    @pl.when(s == n_chunks - 1)
    def _():
        for k in range(CONV_W - 1):
            j = HIST - (CONV_W - 1) + k
            cnew_ref[:, k, :] = ul_ext[j * nb:(j + 1) * nb, :]

    @pl.when(s == pl.num_programs(0) - 1)
    def _():
        for k in range(POOL_MAX_WIN - 1):
            j = HIST - (POOL_MAX_WIN - 1) + k
            pnew_ref[:, k, :] = up_ext[j * nb:(j + 1) * nb, :]


def _mixer(x, h0, cst, pst, win, cw, cb, gwt, gab, gxb, lam, pw, ps, wout, ln_g, ln_b,
           *, layer, state_layer, nb, start_pos):
    n_rows = x.shape[0]
    t_total = n_rows // nb
    tt = min(MIXER_TIME_TILE, t_total)
    assert n_rows % nb == 0 and t_total % tt == 0 and nb % V7X_SUBLANES == 0
    m = tt * nb
    n_chunks = t_total // tt
    n_steps = n_chunks + 1 if n_chunks > 1 else 1
    hr = HIST * nb
    gw = GROUP_WIDTH
    kern = functools.partial(_mixer_kernel, nb=nb, tt=tt, n_chunks=n_chunks, start_pos=start_pos,
                             layer=layer)
    stacked = lambda p: _layer_resident(p.shape, layer)
    out_blk = (lambda i: (jnp.maximum(i - 1, 0), 0)) if n_chunks > 1 else (lambda i: (0, 0))
    return pl.pallas_call(
        kern,
        grid=(n_steps,),
        in_specs=[pl.BlockSpec((m, D_MODEL), lambda i: (jnp.minimum(i, n_chunks - 1), 0)),
                  _layer_resident(h0.shape, state_layer), _layer_resident(cst.shape, state_layer),
                  _layer_resident(pst.shape, state_layer),
                  stacked(win), stacked(cw), _resident(cb.shape), stacked(gwt),
                  _resident(gab.shape), _resident(gxb.shape), _resident(lam.shape),
                  stacked(pw), _resident(ps.shape), stacked(wout), stacked(ln_g), stacked(ln_b)],
        out_specs=[pl.BlockSpec((m, D_MODEL), out_blk),
                   pl.BlockSpec((nb, gw), lambda i: (0, 0)),
                   pl.BlockSpec((nb, CONV_W - 1, gw), lambda i: (0, 0, 0)),
                   pl.BlockSpec((nb, POOL_MAX_WIN - 1, gw), lambda i: (0, 0, 0))],
        out_shape=[jax.ShapeDtypeStruct(x.shape, _F32),
                   jax.ShapeDtypeStruct((nb, gw), _F32),
                   jax.ShapeDtypeStruct((nb, CONV_W - 1, gw), _F32),
                   jax.ShapeDtypeStruct((nb, POOL_MAX_WIN - 1, gw), _F32)],
        scratch_shapes=[pltpu.VMEM((hr + m, gw), _F32),
                        pltpu.VMEM((hr + m, gw), _F32),
                        pltpu.VMEM((nb, gw), _F32),
                        pltpu.VMEM((m, gw), _F32),
                        pltpu.VMEM((m, gw), _F32),
                        pltpu.VMEM((m, gw), _F32),
                        pltpu.VMEM((m, gw), _F32),
                        pltpu.VMEM((m, gw), _F32),
                        pltpu.VMEM((m, D_MODEL), _F32)],
        compiler_params=pltpu.CompilerParams(
            dimension_semantics=("arbitrary",),
            vmem_limit_bytes=V7X_VMEM_LIMIT_BYTES),
        name="mixer",
    )(x, h0, cst, pst, win, cw, cb, gwt, gab, gxb, lam, pw, ps, wout, ln_g, ln_b)


def _block_diag_halves(w):
    groups, n = w.shape[1], w.shape[2]
    gh = groups // 2
    halves = []
    for p in range(2):
        rows = [jnp.pad(w[:, p * gh + g], ((0, 0), (0, 0), (g * n, (gh - 1 - g) * n)))
                for g in range(gh)]
        halves.append(jnp.concatenate(rows, axis=1))
    return jnp.stack(halves, axis=1)


def kernel(x_prompt, x_sample, state_lru_h, state_conv, state_pool, ln_g, ln_b,
           w1_gate, w1_up, w1_down, w_in, conv_w, conv_b, gate_a_w, gate_a_b,
           gate_x_w, gate_x_b, lru_lambda, pool_w, pool_scale, w_out,
           w2_gate, w2_up, w2_down):
    batch, seq, d = x_prompt.shape
    dec_batch = x_sample.shape[0]
    gw = GROUP_WIDTH

    win = w_in.astype(_BF16)
    wout = w_out.astype(_BF16)
    gate_w = jnp.concatenate(
        [_block_diag_halves(gate_a_w), _block_diag_halves(gate_x_w)], axis=-1).astype(_BF16)
    pool_wb = _block_diag_halves(pool_w).astype(_BF16)
    mix_params = (win, conv_w, conv_b, gate_w, gate_a_b, gate_x_b, lru_lambda, pool_wb,
                  pool_scale, wout, ln_g, ln_b)

    xp = x_prompt
    xs = x_sample.reshape(dec_batch, d)
    zeros_h = jnp.zeros((1, batch, gw), _F32)
    zeros_conv = jnp.zeros((1, batch, CONV_W - 1, gw), _F32)
    zeros_pool = jnp.zeros((1, batch, POOL_MAX_WIN - 1, gw), _F32)

    p_h, p_conv, p_pool, s_h, s_conv, s_pool = [], [], [], [], [], []
    for l in range(DEPTH):
        xp, xs = _ffn(xp, xs, w1_gate, w1_up, w1_down, ln_g, ln_b, layer=l, ln_row=0,
                      layout="batch_to_time" if l == 0 else "rows")
        xp = xp.reshape(seq * batch, d)
        xp, h_t, cnew, pnew = _mixer(xp, zeros_h, zeros_conv, zeros_pool, *mix_params,
                                     layer=l, state_layer=0, nb=batch, start_pos=0)
        p_h.append(h_t); p_conv.append(cnew); p_pool.append(pnew)
        xs, h_t, cnew, pnew = _mixer(xs, state_lru_h, state_conv, state_pool, *mix_params,
                                     layer=l, state_layer=l, nb=dec_batch, start_pos=PAST_LEN)
        s_h.append(h_t); s_conv.append(cnew); s_pool.append(pnew)
        if l == DEPTH - 1:
            xp = xp.reshape(seq, batch, d)
        xp, xs = _ffn(xp, xs, w2_gate, w2_up, w2_down, ln_g, ln_b, layer=l, ln_row=2,
                      layout="time_to_batch" if l == DEPTH - 1 else "rows")

    y_sample = xs.reshape(dec_batch, 1, d)
    return (xp, y_sample,
            jnp.stack(p_h), jnp.stack(p_conv), jnp.stack(p_pool),
            jnp.stack(s_h), jnp.stack(s_conv), jnp.stack(s_pool))
```

```python
import functools
import math

import jax
import jax.numpy as jnp
from jax import lax
from jax.experimental import pallas as pl
from jax.experimental.pallas import tpu as pltpu

D_MODEL = 1024
DEPTH = 4
GROUP_WIDTH = 512
LRU_C = 8.0
CONV_W = 4
POOL_WINDOWS = (2, 4, 8, 16)
POOL_MAX_WIN = 16
D_FF = 2816
LN_EPS = 1e-5
DEEPNORM_ALPHA = (2.0 * DEPTH) ** 0.25
PAST_LEN = 16384
assert math.frexp(LRU_C)[0] == 0.5 and all(math.frexp(w)[0] == 0.5 for w in POOL_WINDOWS)

V7X_LANES = 128
V7X_SUBLANES = 8
V7X_MXU_DIM = 256
V7X_VMEM_LIMIT_BYTES = 56 * 1024 * 1024

HIST = POOL_MAX_WIN
FFN_TOKEN_TILE = 512
MIXER_TIME_TILE = 64
FFN_CHUNKS = (768, 768, 768, 512)
assert sum(FFN_CHUNKS) == D_FF and all(c % V7X_MXU_DIM == 0 for c in FFN_CHUNKS)
FFN_LOAD_SLOTS = 2

_BF16 = jnp.bfloat16
_F32 = jnp.float32


def _dot(a, b):
    return jnp.dot(a, b, preferred_element_type=_F32)


def _layernorm(y, g, b):
    mu = jnp.mean(y, axis=-1, keepdims=True)
    d = y - mu
    var = jnp.mean(d * d, axis=-1, keepdims=True)
    return d * lax.rsqrt(var + LN_EPS) * g + b


def _zero_tile_from(v, shape):
    u = lax.bitcast_convert_type(v, jnp.uint32)
    rows, cols = u.shape
    bf16_rows = 2 * V7X_SUBLANES
    acc = u[0:bf16_rows, :]
    for k in range(1, rows // bf16_rows):
        acc = acc | u[k * bf16_rows:(k + 1) * bf16_rows, :]
    word = acc[:, 0:V7X_LANES]
    for k in range(1, cols // V7X_LANES):
        word = word | acc[:, k * V7X_LANES:(k + 1) * V7X_LANES]
    zero = lax.bitcast_convert_type((word >> 16) >> 16, _F32).astype(_BF16)
    zero = jnp.concatenate([zero] * (shape[1] // V7X_LANES), axis=1)
    return jnp.concatenate([zero] * (shape[0] // bf16_rows), axis=0)


def _resident(shape):
    nd = len(shape)
    return pl.BlockSpec(shape, lambda *_: (0,) * nd, pipeline_mode=pl.Buffered(1))


def _layer_resident(shape, layer):
    nd = len(shape)
    return pl.BlockSpec((None,) + tuple(shape[1:]), lambda *_: (layer,) + (0,) * (nd - 1),
                        pipeline_mode=pl.Buffered(1))


def _ffn_weight_loader(layer, wg_hbm, wu_hbm, wd_hbm, wg_v, wu_v, wd_v, st_gu, st_d, sems):
    offsets = [sum(FFN_CHUNKS[:p]) for p in range(len(FFN_CHUNKS))]

    def copies(p):
        slot = p % FFN_LOAD_SLOTS
        cols, head = pl.ds(offsets[p], FFN_CHUNKS[p]), pl.ds(0, FFN_CHUNKS[p])
        return (pltpu.make_async_copy(wg_hbm.at[layer, :, cols], st_gu.at[slot, 0, :, head], sems.at[slot, 0]),
                pltpu.make_async_copy(wu_hbm.at[layer, :, cols], st_gu.at[slot, 1, :, head], sems.at[slot, 1]),
                pltpu.make_async_copy(wd_hbm.at[layer, cols, :], st_d.at[slot, head, :], sems.at[slot, 2]))

    def start(p):
        if p < len(FFN_CHUNKS):
            for cp in copies(p):
                cp.start()

    def finish(p):
        slot = p % FFN_LOAD_SLOTS
        for cp in copies(p):
            cp.wait()
        lo, w = offsets[p], FFN_CHUNKS[p]
        wg_v[:, lo:lo + w] = st_gu[slot, 0, :, 0:w].astype(_BF16)
        wu_v[:, lo:lo + w] = st_gu[slot, 1, :, 0:w].astype(_BF16)
        wd_v[lo:lo + w, :] = st_d[slot, 0:w, :].astype(_BF16)

    return start, finish


def _swiglu(xb, wg_v, wu_v, wd_v, per_chunk=None, before_chunk=None):
    acc = jnp.zeros((xb.shape[0], D_MODEL), _F32)
    lo = 0
    for ci, width in enumerate(FFN_CHUNKS):
        if before_chunk is not None:
            before_chunk(ci)
        gate = _dot(xb, wg_v[:, lo:lo + width])
        up = _dot(xb, wu_v[:, lo:lo + width])
        hid = (jax.nn.silu(gate) * up).astype(_BF16)
        if per_chunk is not None:
            hid = per_chunk(ci, hid)
        acc = acc + _dot(hid, wd_v[lo:lo + width, :])
        lo += width
    return acc


def _ffn_kernel(xp_ref, xs_ref, wg_hbm, wu_hbm, wd_hbm, g_ref, b_ref, op_ref, os_ref,
                wg_v, wu_v, wd_v, st_gu, st_d, sems, y_scr, ys_scr,
                *, layer, ln_row, n_tiles, layout):
    s = pl.program_id(0)
    ln = lambda y: _layernorm(y, g_ref[ln_row:ln_row + 1, :], b_ref[ln_row:ln_row + 1, :])

    if layout == "rows":
        load_tile = lambda: xp_ref[...]
        def store_rows(r0, r1, val):
            op_ref[r0:r1, :] = val
    elif layout == "batch_to_time":
        nbat, tt = xp_ref.shape[0], xp_ref.shape[1]
        load_tile = lambda: xp_ref[...].reshape(nbat * tt, D_MODEL)
        def store_rows(r0, r1, val):
            for b in range(r0 // tt, r1 // tt):
                op_ref[:, b, :] = val[b * tt - r0:(b + 1) * tt - r0, :]
    else:
        tt, nbat = xp_ref.shape[0], xp_ref.shape[1]
        load_tile = lambda: jnp.concatenate([xp_ref[:, b, :] for b in range(nbat)], axis=0)
        def store_rows(r0, r1, val):
            op_ref[r0 // tt:r1 // tt, :, :] = val.reshape((r1 - r0) // tt, tt, D_MODEL)

    def prompt_tile(before_chunk=None):
        x = load_tile()
        part = x.shape[0] // len(FFN_CHUNKS)

        def ln_part(ci, hid):
            rows = slice(ci * part, (ci + 1) * part)
            normed = ln(y_scr[rows, :])
            store_rows(ci * part, (ci + 1) * part, normed)
            return hid + _zero_tile_from(normed, hid.shape)

        acc = _swiglu(x.astype(_BF16), wg_v, wu_v, wd_v, ln_part, before_chunk)
        y_scr[...] = DEEPNORM_ALPHA * x + 0.5 * acc

    @pl.when(s == 0)
    def _():
        y_scr[...] = jnp.zeros(y_scr.shape, _F32)
        start, finish = _ffn_weight_loader(layer, wg_hbm, wu_hbm, wd_hbm, wg_v, wu_v, wd_v,
                                           st_gu, st_d, sems)
        for p in range(FFN_LOAD_SLOTS):
            start(p)

        def load_chunk(ci):
            finish(ci)
            start(ci + FFN_LOAD_SLOTS)

        prompt_tile(load_chunk)

    @pl.when((s > 0) & (s < n_tiles))
    def _():
        prompt_tile()

    @pl.when(s == n_tiles)
    def _():
        store_rows(0, y_scr.shape[0], ln(y_scr[...]))
        xs = xs_ref[...]
        ys_scr[...] = DEEPNORM_ALPHA * xs + 0.5 * _swiglu(xs.astype(_BF16), wg_v, wu_v, wd_v)

    @pl.when(s == n_tiles + 1)
    def _():
        os_ref[...] = ln(ys_scr[...])


def _ffn(xp, xs, wg, wu, wd, ln_g, ln_b, *, layer, ln_row, layout="rows"):
    ms = xs.shape[0]
    tm = FFN_TOKEN_TILE
    assert tm % (len(FFN_CHUNKS) * 2 * V7X_SUBLANES) == 0
    if layout == "rows":
        m = xp.shape[0]
        in_block, out_block, out_full = (tm, D_MODEL), (tm, D_MODEL), (m, D_MODEL)
        at = lambda j: (j, 0)
        in_at = out_at = at
    else:
        nbat, t_total = (xp.shape[0], xp.shape[1]) if layout == "batch_to_time" else (xp.shape[1], xp.shape[0])
        m = nbat * t_total
        tt = tm // nbat
        part_rows = tm // len(FFN_CHUNKS)
        assert tm % nbat == 0 and tt % V7X_SUBLANES == 0 and part_rows % tt == 0
        batch_major = ((nbat, tt, D_MODEL), lambda j: (0, j, 0), (nbat, t_total, D_MODEL))
        time_major = ((tt, nbat, D_MODEL), lambda j: (j, 0, 0), (t_total, nbat, D_MODEL))
        src, dst = (batch_major, time_major) if layout == "batch_to_time" else (time_major, batch_major)
        in_block, in_at = src[0], src[1]
        out_block, out_at, out_full = dst
    assert m % tm == 0
    n_tiles = m // tm
    hbm = pl.BlockSpec(memory_space=pl.ANY)
    return pl.pallas_call(
        functools.partial(_ffn_kernel, layer=layer, ln_row=ln_row, n_tiles=n_tiles, layout=layout),
        grid=(n_tiles + 2,),
        in_specs=[
            pl.BlockSpec(in_block, lambda i: in_at(jnp.minimum(i, n_tiles - 1))),
            pl.BlockSpec((ms, D_MODEL), lambda i: (0, 0)),
            hbm, hbm, hbm,
            _layer_resident(ln_g.shape, layer), _layer_resident(ln_b.shape, layer),
        ],
        out_specs=[pl.BlockSpec(out_block, lambda i: out_at(jnp.clip(i - 1, 0, n_tiles - 1))),
                   pl.BlockSpec((ms, D_MODEL), lambda i: (0, 0))],
        out_shape=[jax.ShapeDtypeStruct(out_full, _F32),
                   jax.ShapeDtypeStruct((ms, D_MODEL), _F32)],
        scratch_shapes=[pltpu.VMEM((D_MODEL, D_FF), _BF16),
                        pltpu.VMEM((D_MODEL, D_FF), _BF16),
                        pltpu.VMEM((D_FF, D_MODEL), _BF16),
                        pltpu.VMEM((FFN_LOAD_SLOTS, 2, D_MODEL, max(FFN_CHUNKS)), _F32),
                        pltpu.VMEM((FFN_LOAD_SLOTS, max(FFN_CHUNKS), D_MODEL), _F32),
                        pltpu.SemaphoreType.DMA((FFN_LOAD_SLOTS, 3)),
                        pltpu.VMEM((tm, D_MODEL), _F32),
                        pltpu.VMEM((ms, D_MODEL), _F32)],
        compiler_params=pltpu.CompilerParams(
            dimension_semantics=("arbitrary",),
            vmem_limit_bytes=V7X_VMEM_LIMIT_BYTES),
        name="ffn",
    )(xp, xs, wg, wu, wd, ln_g, ln_b)


def _mixer_kernel(x_ref, h0_ref, cst_ref, pst_ref, win_ref, cw_ref, cb_ref, gw_ref, gab_ref,
                  gxb_ref, lam_ref, pw_ref, ps_ref, wout_ref, g_ref, b_ref,
                  o_ref, ht_ref, cnew_ref, pnew_ref,
                  ul_ext, up_ext, h_scr, ga_scr, gx_scr, xc_scr, hs_scr, gate_scr, x_scr,
                  *, nb, tt, n_chunks, start_pos, layer):
    s = pl.program_id(0)
    pipelined = n_chunks > 1
    m = tt * nb
    hr = HIST * nb
    gw = GROUP_WIDTH
    half = gw // 2
    vec = lambda ref: ref[layer:layer + 1, :]

    @pl.when(s == 0)
    def _():
        h_scr[...] = h0_ref[...]
        ul_ext[0:hr, :] = jnp.zeros((hr, gw), _F32)
        for k in range(CONV_W - 1):
            j = HIST - (CONV_W - 1) + k
            ul_ext[j * nb:(j + 1) * nb, :] = cst_ref[:, k, :]
        up_ext[0:nb, :] = jnp.zeros((nb, gw), _F32)
        for k in range(POOL_MAX_WIN - 1):
            j = HIST - (POOL_MAX_WIN - 1) + k
            up_ext[j * nb:(j + 1) * nb, :] = pst_ref[:, k, :]

    def project():
        x = x_ref[...]
        return x, _dot(x.astype(_BF16), win_ref[...])

    def conv_and_gates(proj):
        u_lru = proj[:, 0:gw]
        gate_scr[...] = proj[:, gw:2 * gw]
        ul_ext[hr:hr + m, :] = u_lru
        up_ext[hr:hr + m, :] = proj[:, 2 * gw:3 * gw]
        xc = vec(cb_ref) + cw_ref[CONV_W - 1:CONV_W, :] * u_lru
        for k in range(CONV_W - 1):
            off = hr - (CONV_W - 1 - k) * nb
            xc = xc + cw_ref[k:k + 1, :] * ul_ext[off:off + m, :]
        xc_scr[...] = xc
        xcb = xc.astype(_BF16)
        for p in range(2):
            cols = slice(p * half, (p + 1) * half)
            gts = _dot(xcb[:, cols], gw_ref[p])
            ga_scr[:, cols] = gts[:, 0:half]
            gx_scr[:, cols] = gts[:, half:2 * half]
        ul_ext[0:hr, :] = ul_ext[m:m + hr, :]

    def recurrence_and_mix(chunk):
        xc = xc_scr[...]
        r = jax.nn.sigmoid(ga_scr[...] + vec(gab_ref))
        i = jax.nn.sigmoid(gx_scr[...] + vec(gxb_ref))
        nl = -vec(lam_ref)
        softplus_nl = jnp.maximum(nl, 0.0) + jnp.log1p(jnp.exp(-jnp.abs(nl)))
        log_a = r * (-LRU_C * softplus_nl)
        a = jnp.exp(log_a)
        one_m_a2 = 1.0 - a * a
        mult = jnp.where(one_m_a2 > 0.0, one_m_a2 * lax.rsqrt(one_m_a2), one_m_a2)
        uu = mult * (i * xc)
        h = h_scr[...]
        for t in range(tt):
            rows = slice(t * nb, (t + 1) * nb)
            h = a[rows, :] * h + uu[rows, :]
            hs_scr[rows, :] = h
        h_scr[...] = h
        ht_ref[...] = h
        y_lru = hs_scr[...] * jax.nn.gelu(gate_scr[...])

        t_loc = lax.broadcasted_iota(jnp.int32, (tt, nb, V7X_LANES), 0).reshape(m, V7X_LANES)
        pos = start_pos + chunk * tt + t_loc
        pooled = []
        for gi, w in enumerate(POOL_WINDOWS):
            lanes = slice(gi * V7X_LANES, (gi + 1) * V7X_LANES)
            n_rows = (w - 1) * nb + m
            sums = up_ext[hr - (w - 1) * nb:hr + m, lanes]
            span = 1
            while span < w:
                n_rows -= span * nb
                sums = sums[span * nb:span * nb + n_rows, :] + sums[0:n_rows, :]
                span *= 2
            head = min(m, (w - 1) * nb)
            cnt = jnp.minimum(w, pos[0:head, :] + 1).astype(_F32)
            mean = sums[0:head, :] / cnt
            if head < m:
                mean = jnp.concatenate([mean, sums[head:m, :] * (1.0 / w)], axis=0)
            pooled.append(mean - up_ext[hr:hr + m, lanes])
        pooled = jnp.concatenate(pooled, axis=-1).astype(_BF16)
        y_pool = jnp.concatenate(
            [_dot(pooled[:, p * half:(p + 1) * half], pw_ref[p]) for p in range(2)],
            axis=-1) * vec(ps_ref)

        mix = _dot(jnp.concatenate([y_lru, y_pool], axis=-1).astype(_BF16), wout_ref[...])
        up_ext[0:hr, :] = up_ext[m:m + hr, :]
        return mix

    def residual_norm(mix):
        y = DEEPNORM_ALPHA * x_scr[...] + mix
        o_ref[...] = _layernorm(y, g_ref[1:2, :], b_ref[1:2, :])

    def stage_a():
        x, proj = project()
        x_scr[...] = x
        conv_and_gates(proj)

    if pipelined:
        @pl.when(s == 0)
        def _():
            stage_a()

        @pl.when((s > 0) & (s < n_chunks))
        def _():
            x, proj = project()
            mix = recurrence_and_mix(s - 1)
            conv_and_gates(proj)
            residual_norm(mix)
            x_scr[...] = x

        @pl.when(s == n_chunks)
        def _():
            residual_norm(recurrence_and_mix(s - 1))
    else:
        stage_a()
        residual_norm(recurrence_and_mix(s))

    @pl.when(s == n_chunks - 1)
    def _():
        for k in range(CONV_W - 1):
            j = HIST - (CONV_W - 1) + k
            cnew_ref[:, k, :] = ul_ext[j * nb:(j + 1) * nb, :]

    @pl.when(s == pl.num_programs(0) - 1)
    def _():
        for k in range(POOL_MAX_WIN - 1):
            j = HIST - (POOL_MAX_WIN - 1) + k
            pnew_ref[:, k, :] = up_ext[j * nb:(j + 1) * nb, :]


def _mixer(x, h0, cst, pst, win, cw, cb, gwt, gab, gxb, lam, pw, ps, wout, ln_g, ln_b,
           *, layer, state_layer, nb, start_pos):
    n_rows = x.shape[0]
    t_total = n_rows // nb
    tt = min(MIXER_TIME_TILE, t_total)
    assert n_rows % nb == 0 and t_total % tt == 0 and nb % V7X_SUBLANES == 0
    m = tt * nb
    n_chunks = t_total // tt
    n_steps = n_chunks + 1 if n_chunks > 1 else 1
    hr = HIST * nb
    gw = GROUP_WIDTH
    kern = functools.partial(_mixer_kernel, nb=nb, tt=tt, n_chunks=n_chunks, start_pos=start_pos,
                             layer=layer)
    stacked = lambda p: _layer_resident(p.shape, layer)
    out_blk = (lambda i: (jnp.maximum(i - 1, 0), 0)) if n_chunks > 1 else (lambda i: (0, 0))
    return pl.pallas_call(
        kern,
        grid=(n_steps,),
        in_specs=[pl.BlockSpec((m, D_MODEL), lambda i: (jnp.minimum(i, n_chunks - 1), 0)),
                  _layer_resident(h0.shape, state_layer), _layer_resident(cst.shape, state_layer),
                  _layer_resident(pst.shape, state_layer),
                  stacked(win), stacked(cw), _resident(cb.shape), stacked(gwt),
                  _resident(gab.shape), _resident(gxb.shape), _resident(lam.shape),
                  stacked(pw), _resident(ps.shape), stacked(wout), stacked(ln_g), stacked(ln_b)],
        out_specs=[pl.BlockSpec((m, D_MODEL), out_blk),
                   pl.BlockSpec((nb, gw), lambda i: (0, 0)),
                   pl.BlockSpec((nb, CONV_W - 1, gw), lambda i: (0, 0, 0)),
                   pl.BlockSpec((nb, POOL_MAX_WIN - 1, gw), lambda i: (0, 0, 0))],
        out_shape=[jax.ShapeDtypeStruct(x.shape, _F32),
                   jax.ShapeDtypeStruct((nb, gw), _F32),
                   jax.ShapeDtypeStruct((nb, CONV_W - 1, gw), _F32),
                   jax.ShapeDtypeStruct((nb, POOL_MAX_WIN - 1, gw), _F32)],
        scratch_shapes=[pltpu.VMEM((hr + m, gw), _F32),
                        pltpu.VMEM((hr + m, gw), _F32),
                        pltpu.VMEM((nb, gw), _F32),
                        pltpu.VMEM((m, gw), _F32),
                        pltpu.VMEM((m, gw), _F32),
                        pltpu.VMEM((m, gw), _F32),
                        pltpu.VMEM((m, gw), _F32),
                        pltpu.VMEM((m, gw), _F32),
                        pltpu.VMEM((m, D_MODEL), _F32)],
        compiler_params=pltpu.CompilerParams(
            dimension_semantics=("arbitrary",),
            vmem_limit_bytes=V7X_VMEM_LIMIT_BYTES),
        name="mixer",
    )(x, h0, cst, pst, win, cw, cb, gwt, gab, gxb, lam, pw, ps, wout, ln_g, ln_b)


def _block_diag_halves(w):
    groups, n = w.shape[1], w.shape[2]
    gh = groups // 2
    halves = []
    for p in range(2):
        rows = [jnp.pad(w[:, p * gh + g], ((0, 0), (0, 0), (g * n, (gh - 1 - g) * n)))
                for g in range(gh)]
        halves.append(jnp.concatenate(rows, axis=1))
    return jnp.stack(halves, axis=1)


def kernel(x_prompt, x_sample, state_lru_h, state_conv, state_pool, ln_g, ln_b,
           w1_gate, w1_up, w1_down, w_in, conv_w, conv_b, gate_a_w, gate_a_b,
           gate_x_w, gate_x_b, lru_lambda, pool_w, pool_scale, w_out,
           w2_gate, w2_up, w2_down):
    batch, seq, d = x_prompt.shape
    dec_batch = x_sample.shape[0]
    gw = GROUP_WIDTH

    win = w_in.astype(_BF16)
    wout = w_out.astype(_BF16)
    gate_w = jnp.concatenate(
        [_block_diag_halves(gate_a_w), _block_diag_halves(gate_x_w)], axis=-1).astype(_BF16)
    pool_wb = _block_diag_halves(pool_w).astype(_BF16)
    mix_params = (win, conv_w, conv_b, gate_w, gate_a_b, gate_x_b, lru_lambda, pool_wb,
                  pool_scale, wout, ln_g, ln_b)

    xp = x_prompt
    xs = x_sample.reshape(dec_batch, d)
    zeros_h = jnp.zeros((1, batch, gw), _F32)
    zeros_conv = jnp.zeros((1, batch, CONV_W - 1, gw), _F32)
    zeros_pool = jnp.zeros((1, batch, POOL_MAX_WIN - 1, gw), _F32)

    p_h, p_conv, p_pool, s_h, s_conv, s_pool = [], [], [], [], [], []
    for l in range(DEPTH):
        xp, xs = _ffn(xp, xs, w1_gate, w1_up, w1_down, ln_g, ln_b, layer=l, ln_row=0,
                      layout="batch_to_time" if l == 0 else "rows")
        xp = xp.reshape(seq * batch, d)
        xp, h_t, cnew, pnew = _mixer(xp, zeros_h, zeros_conv, zeros_pool, *mix_params,
                                     layer=l, state_layer=0, nb=batch, start_pos=0)
        p_h.append(h_t); p_conv.append(cnew); p_pool.append(pnew)
        xs, h_t, cnew, pnew = _mixer(xs, state_lru_h, state_conv, state_pool, *mix_params,
                                     layer=l, state_layer=l, nb=dec_batch, start_pos=PAST_LEN)
        s_h.append(h_t); s_conv.append(cnew); s_pool.append(pnew)
        if l == DEPTH - 1:
            xp = xp.reshape(seq, batch, d)
        xp, xs = _ffn(xp, xs, w2_gate, w2_up, w2_down, ln_g, ln_b, layer=l, ln_row=2,
                      layout="time_to_batch" if l == DEPTH - 1 else "rows")

    y_sample = xs.reshape(dec_batch, 1, d)
    return (xp, y_sample,
            jnp.stack(p_h), jnp.stack(p_conv), jnp.stack(p_pool),
            jnp.stack(s_h), jnp.stack(s_conv), jnp.stack(s_pool))
```

```python
import functools
import math

import jax
import jax.numpy as jnp
from jax import lax
from jax.experimental import pallas as pl
from jax.experimental.pallas import tpu as pltpu

D_MODEL = 1024
DEPTH = 4
GROUP_WIDTH = 512
LRU_C = 8.0
CONV_W = 4
POOL_WINDOWS = (2, 4, 8, 16)
POOL_MAX_WIN = 16
D_FF = 2816
LN_EPS = 1e-5
DEEPNORM_ALPHA = (2.0 * DEPTH) ** 0.25
PAST_LEN = 16384
assert math.frexp(LRU_C)[0] == 0.5 and all(math.frexp(w)[0] == 0.5 for w in POOL_WINDOWS)

V7X_LANES = 128
V7X_SUBLANES = 8
V7X_MXU_DIM = 256
V7X_VMEM_LIMIT_BYTES = 56 * 1024 * 1024

HIST = POOL_MAX_WIN
FFN_TOKEN_TILE = 512
MIXER_TIME_TILE = 64
FFN_CHUNKS = (768, 768, 768, 512)
assert sum(FFN_CHUNKS) == D_FF and all(c % V7X_MXU_DIM == 0 for c in FFN_CHUNKS)
FFN_LOAD_SLOTS = 2

_BF16 = jnp.bfloat16
_F32 = jnp.float32


def _dot(a, b):
    return jnp.dot(a, b, preferred_element_type=_F32)


def _layernorm(y, g, b):
    mu = jnp.mean(y, axis=-1, keepdims=True)
    d = y - mu
    var = jnp.mean(d * d, axis=-1, keepdims=True)
    return d * lax.rsqrt(var + LN_EPS) * g + b


def _zero_tile_from(v, shape):
    u = lax.bitcast_convert_type(v, jnp.uint32)
    rows, cols = u.shape
    bf16_rows = 2 * V7X_SUBLANES
    acc = u[0:bf16_rows, :]
    for k in range(1, rows // bf16_rows):
        acc = acc | u[k * bf16_rows:(k + 1) * bf16_rows, :]
    word = acc[:, 0:V7X_LANES]
    for k in range(1, cols // V7X_LANES):
        word = word | acc[:, k * V7X_LANES:(k + 1) * V7X_LANES]
    zero = lax.bitcast_convert_type((word >> 16) >> 16, _F32).astype(_BF16)
    zero = jnp.concatenate([zero] * (shape[1] // V7X_LANES), axis=1)
    return jnp.concatenate([zero] * (shape[0] // bf16_rows), axis=0)


def _resident(shape):
    nd = len(shape)
    return pl.BlockSpec(shape, lambda *_: (0,) * nd, pipeline_mode=pl.Buffered(1))


def _layer_resident(shape, layer):
    nd = len(shape)
    return pl.BlockSpec((None,) + tuple(shape[1:]), lambda *_: (layer,) + (0,) * (nd - 1),
                        pipeline_mode=pl.Buffered(1))


def _ffn_weight_loader(layer, wg_hbm, wu_hbm, wd_hbm, wg_v, wu_v, wd_v, st_gu, st_d, sems):
    offsets = [sum(FFN_CHUNKS[:p]) for p in range(len(FFN_CHUNKS))]

    def copies(p):
        slot = p % FFN_LOAD_SLOTS
        cols, head = pl.ds(offsets[p], FFN_CHUNKS[p]), pl.ds(0, FFN_CHUNKS[p])
        return (pltpu.make_async_copy(wg_hbm.at[layer, :, cols], st_gu.at[slot, 0, :, head], sems.at[slot, 0]),
                pltpu.make_async_copy(wu_hbm.at[layer, :, cols], st_gu.at[slot, 1, :, head], sems.at[slot, 1]),
                pltpu.make_async_copy(wd_hbm.at[layer, cols, :], st_d.at[slot, head, :], sems.at[slot, 2]))

    def start(p):
        if p < len(FFN_CHUNKS):
            for cp in copies(p):
                cp.start()

    def finish(p):
        slot = p % FFN_LOAD_SLOTS
        for cp in copies(p):
            cp.wait()
        lo, w = offsets[p], FFN_CHUNKS[p]
        wg_v[:, lo:lo + w] = st_gu[slot, 0, :, 0:w].astype(_BF16)
        wu_v[:, lo:lo + w] = st_gu[slot, 1, :, 0:w].astype(_BF16)
        wd_v[lo:lo + w, :] = st_d[slot, 0:w, :].astype(_BF16)

    return start, finish


def _swiglu(xb, wg_v, wu_v, wd_v, per_chunk=None, before_chunk=None):
    acc = jnp.zeros((xb.shape[0], D_MODEL), _F32)
    lo = 0
    for ci, width in enumerate(FFN_CHUNKS):
        if before_chunk is not None:
            before_chunk(ci)
        gate = _dot(xb, wg_v[:, lo:lo + width])
        up = _dot(xb, wu_v[:, lo:lo + width])
        hid = (jax.nn.silu(gate) * up).astype(_BF16)
        if per_chunk is not None:
            hid = per_chunk(ci, hid)
        acc = acc + _dot(hid, wd_v[lo:lo + width, :])
        lo += width
    return acc


def _ffn_kernel(xp_ref, xs_ref, wg_hbm, wu_hbm, wd_hbm, g_ref, b_ref, op_ref, os_ref,
                wg_v, wu_v, wd_v, st_gu, st_d, sems, y_scr, ys_scr,
                *, layer, ln_row, n_tiles, layout):
    s = pl.program_id(0)
    ln = lambda y: _layernorm(y, g_ref[ln_row:ln_row + 1, :], b_ref[ln_row:ln_row + 1, :])

    if layout == "rows":
        load_tile = lambda: xp_ref[...]
        def store_rows(r0, r1, val):
            op_ref[r0:r1, :] = val
    elif layout == "batch_to_time":
        nbat, tt = xp_ref.shape[0], xp_ref.shape[1]
        load_tile = lambda: xp_ref[...].reshape(nbat * tt, D_MODEL)
        def store_rows(r0, r1, val):
            for b in range(r0 // tt, r1 // tt):
                op_ref[:, b, :] = val[b * tt - r0:(b + 1) * tt - r0, :]
    else:
        tt, nbat = xp_ref.shape[0], xp_ref.shape[1]
        load_tile = lambda: jnp.concatenate([xp_ref[:, b, :] for b in range(nbat)], axis=0)
        def store_rows(r0, r1, val):
            op_ref[r0 // tt:r1 // tt, :, :] = val.reshape((r1 - r0) // tt, tt, D_MODEL)

    def prompt_tile(before_chunk=None):
        x = load_tile()
        part = x.shape[0] // len(FFN_CHUNKS)

        def ln_part(ci, hid):
            rows = slice(ci * part, (ci + 1) * part)
            normed = ln(y_scr[rows, :])
            store_rows(ci * part, (ci + 1) * part, normed)
            return hid + _zero_tile_from(normed, hid.shape)

        acc = _swiglu(x.astype(_BF16), wg_v, wu_v, wd_v, ln_part, before_chunk)
        y_scr[...] = DEEPNORM_ALPHA * x + 0.5 * acc

    @pl.when(s == 0)
    def _():
        y_scr[...] = jnp.zeros(y_scr.shape, _F32)
        start, finish = _ffn_weight_loader(layer, wg_hbm, wu_hbm, wd_hbm, wg_v, wu_v, wd_v,
                                           st_gu, st_d, sems)
        for p in range(FFN_LOAD_SLOTS):
            start(p)

        def load_chunk(ci):
            finish(ci)
            start(ci + FFN_LOAD_SLOTS)

        prompt_tile(load_chunk)

    @pl.when((s > 0) & (s < n_tiles))
    def _():
        prompt_tile()

    @pl.when(s == n_tiles)
    def _():
        store_rows(0, y_scr.shape[0], ln(y_scr[...]))
        xs = xs_ref[...]
        ys_scr[...] = DEEPNORM_ALPHA * xs + 0.5 * _swiglu(xs.astype(_BF16), wg_v, wu_v, wd_v)

    @pl.when(s == n_tiles + 1)
    def _():
        os_ref[...] = ln(ys_scr[...])


def _ffn(xp, xs, wg, wu, wd, ln_g, ln_b, *, layer, ln_row, layout="rows"):
    ms = xs.shape[0]
    tm = FFN_TOKEN_TILE
    assert tm % (len(FFN_CHUNKS) * 2 * V7X_SUBLANES) == 0
    if layout == "rows":
        m = xp.shape[0]
        in_block, out_block, out_full = (tm, D_MODEL), (tm, D_MODEL), (m, D_MODEL)
        at = lambda j: (j, 0)
        in_at = out_at = at
    else:
        nbat, t_total = (xp.shape[0], xp.shape[1]) if layout == "batch_to_time" else (xp.shape[1], xp.shape[0])
        m = nbat * t_total
        tt = tm // nbat
        part_rows = tm // len(FFN_CHUNKS)
        assert tm % nbat == 0 and tt % V7X_SUBLANES == 0 and part_rows % tt == 0
        batch_major = ((nbat, tt, D_MODEL), lambda j: (0, j, 0), (nbat, t_total, D_MODEL))
        time_major = ((tt, nbat, D_MODEL), lambda j: (j, 0, 0), (t_total, nbat, D_MODEL))
        src, dst = (batch_major, time_major) if layout == "batch_to_time" else (time_major, batch_major)
        in_block, in_at = src[0], src[1]
        out_block, out_at, out_full = dst
    assert m % tm == 0
    n_tiles = m // tm
    hbm = pl.BlockSpec(memory_space=pl.ANY)
    return pl.pallas_call(
        functools.partial(_ffn_kernel, layer=layer, ln_row=ln_row, n_tiles=n_tiles, layout=layout),
        grid=(n_tiles + 2,),
        in_specs=[
            pl.BlockSpec(in_block, lambda i: in_at(jnp.minimum(i, n_tiles - 1))),
            pl.BlockSpec((ms, D_MODEL), lambda i: (0, 0)),
            hbm, hbm, hbm,
            _layer_resident(ln_g.shape, layer), _layer_resident(ln_b.shape, layer),
        ],
        out_specs=[pl.BlockSpec(out_block, lambda i: out_at(jnp.clip(i - 1, 0, n_tiles - 1))),
                   pl.BlockSpec((ms, D_MODEL), lambda i: (0, 0))],
        out_shape=[jax.ShapeDtypeStruct(out_full, _F32),
                   jax.ShapeDtypeStruct((ms, D_MODEL), _F32)],
        scratch_shapes=[pltpu.VMEM((D_MODEL, D_FF), _BF16),
                        pltpu.VMEM((D_MODEL, D_FF), _BF16),
                        pltpu.VMEM((D_FF, D_MODEL), _BF16),
                        pltpu.VMEM((FFN_LOAD_SLOTS, 2, D_MODEL, max(FFN_CHUNKS)), _F32),
                        pltpu.VMEM((FFN_LOAD_SLOTS, max(FFN_CHUNKS), D_MODEL), _F32),
                        pltpu.SemaphoreType.DMA((FFN_LOAD_SLOTS, 3)),
                        pltpu.VMEM((tm, D_MODEL), _F32),
                        pltpu.VMEM((ms, D_MODEL), _F32)],
        compiler_params=pltpu.CompilerParams(
            dimension_semantics=("arbitrary",),
            vmem_limit_bytes=V7X_VMEM_LIMIT_BYTES),
        name="ffn",
    )(xp, xs, wg, wu, wd, ln_g, ln_b)


def _mixer_kernel(x_ref, h0_ref, cst_ref, pst_ref, win_ref, cw_ref, cb_ref, gw_ref, gab_ref,
                  gxb_ref, lam_ref, pw_ref, ps_ref, wout_ref, g_ref, b_ref,
                  o_ref, ht_ref, cnew_ref, pnew_ref,
                  ul_ext, up_ext, h_scr, ga_scr, gx_scr, xc_scr, hs_scr, gate_scr, x_scr,
                  *, nb, tt, n_chunks, start_pos, layer):
    s = pl.program_id(0)
    pipelined = n_chunks > 1
    m = tt * nb
    hr = HIST * nb
    gw = GROUP_WIDTH
    half = gw // 2
    vec = lambda ref: ref[layer:layer + 1, :]

    @pl.when(s == 0)
    def _():
        h_scr[...] = h0_ref[...]
        ul_ext[0:hr, :] = jnp.zeros((hr, gw), _F32)
        for k in range(CONV_W - 1):
            j = HIST - (CONV_W - 1) + k
            ul_ext[j * nb:(j + 1) * nb, :] = cst_ref[:, k, :]
        up_ext[0:nb, :] = jnp.zeros((nb, gw), _F32)
        for k in range(POOL_MAX_WIN - 1):
            j = HIST - (POOL_MAX_WIN - 1) + k
            up_ext[j * nb:(j + 1) * nb, :] = pst_ref[:, k, :]

    def project():
        x = x_ref[...]
        return x, _dot(x.astype(_BF16), win_ref[...])

    def conv_and_gates(proj):
        u_lru = proj[:, 0:gw]
        gate_scr[...] = proj[:, gw:2 * gw]
        ul_ext[hr:hr + m, :] = u_lru
        up_ext[hr:hr + m, :] = proj[:, 2 * gw:3 * gw]
        xc = vec(cb_ref) + cw_ref[CONV_W - 1:CONV_W, :] * u_lru
        for k in range(CONV_W - 1):
            off = hr - (CONV_W - 1 - k) * nb
            xc = xc + cw_ref[k:k + 1, :] * ul_ext[off:off + m, :]
        xc_scr[...] = xc
        xcb = xc.astype(_BF16)
        for p in range(2):
            cols = slice(p * half, (p + 1) * half)
            gts = _dot(xcb[:, cols], gw_ref[p])
            ga_scr[:, cols] = gts[:, 0:half]
            gx_scr[:, cols] = gts[:, half:2 * half]
        ul_ext[0:hr, :] = ul_ext[m:m + hr, :]

    def recur_rows(chunk, t0, t1, h):
        r0, r1 = t0 * nb, t1 * nb
        n = r1 - r0
        xc = xc_scr[r0:r1, :]
        r = jax.nn.sigmoid(ga_scr[r0:r1, :] + vec(gab_ref))
        i = jax.nn.sigmoid(gx_scr[r0:r1, :] + vec(gxb_ref))
        nl = -vec(lam_ref)
        softplus_nl = jnp.maximum(nl, 0.0) + jnp.log1p(jnp.exp(-jnp.abs(nl)))
        log_a = r * (-LRU_C * softplus_nl)
        a = jnp.exp(log_a)
        one_m_a2 = 1.0 - a * a
        mult = jnp.where(one_m_a2 > 0.0, one_m_a2 * lax.rsqrt(one_m_a2), one_m_a2)
        uu = mult * (i * xc)
        for t in range(t1 - t0):
            rows = slice(t * nb, (t + 1) * nb)
            h = a[rows, :] * h + uu[rows, :]
            hs_scr[r0 + t * nb:r0 + (t + 1) * nb, :] = h
        y_lru = hs_scr[r0:r1, :] * jax.nn.gelu(gate_scr[r0:r1, :])

        t_loc = lax.broadcasted_iota(jnp.int32, (t1 - t0, nb, V7X_LANES), 0).reshape(n, V7X_LANES)
        pos = start_pos + chunk * tt + t0 + t_loc
        pooled = []
        for gi, w in enumerate(POOL_WINDOWS):
            lanes = slice(gi * V7X_LANES, (gi + 1) * V7X_LANES)
            n_rows = (w - 1) * nb + n
            sums = up_ext[hr + r0 - (w - 1) * nb:hr + r1, lanes]
            span = 1
            while span < w:
                n_rows -= span * nb
                sums = sums[span * nb:span * nb + n_rows, :] + sums[0:n_rows, :]
                span *= 2
            head = min(max(w - 1 - t0, 0), t1 - t0) * nb
            parts = []
            if head > 0:
                cnt = jnp.minimum(w, pos[0:head, :] + 1).astype(_F32)
                parts.append(sums[0:head, :] / cnt)
            if head < n:
                parts.append(sums[head:n, :] * (1.0 / w))
            mean = parts[0] if len(parts) == 1 else jnp.concatenate(parts, axis=0)
            pooled.append(mean - up_ext[hr + r0:hr + r1, lanes])
        return h, y_lru, jnp.concatenate(pooled, axis=-1)

    def mix_rows(y_lru, pooled):
        pooled = pooled.astype(_BF16)
        y_pool = jnp.concatenate(
            [_dot(pooled[:, p * half:(p + 1) * half], pw_ref[p]) for p in range(2)],
            axis=-1) * vec(ps_ref)

        mix = _dot(jnp.concatenate([y_lru, y_pool], axis=-1).astype(_BF16), wout_ref[...])
        up_ext[0:hr, :] = up_ext[m:m + hr, :]
        return mix

    def residual_norm(mix):
        y = DEEPNORM_ALPHA * x_scr[...] + mix
        o_ref[...] = _layernorm(y, g_ref[1:2, :], b_ref[1:2, :])

    def stage_a():
        x, proj = project()
        x_scr[...] = x
        conv_and_gates(proj)

    def keep_state(h):
        h_scr[...] = h
        ht_ref[...] = h

    def recurrence_and_mix(chunk):
        h, y_lru, pooled = recur_rows(chunk, 0, tt, h_scr[...])
        keep_state(h)
        return mix_rows(y_lru, pooled)

    if pipelined:
        @pl.when(s == 0)
        def _():
            stage_a()

        @pl.when((s > 0) & (s < n_chunks))
        def _():
            n_proj = 3
            x = x_ref[...]
            xb = x.astype(_BF16)
            bounds = [2 * ((k * tt) // (2 * n_proj)) for k in range(n_proj)] + [tt]
            h, xk, proj, y_lru, pooled = h_scr[...], xb, [], [], []
            for k in range(n_proj):
                proj.append(_dot(xk, win_ref[:, k * gw:(k + 1) * gw]))
                h, y_k, p_k = recur_rows(s - 1, bounds[k], bounds[k + 1], h)
                y_lru.append(y_k)
                pooled.append(p_k)
                if k + 2 == n_proj:
                    done = jnp.concatenate([jnp.concatenate(y_lru, axis=0),
                                            jnp.concatenate(pooled, axis=0)], axis=1)
                    xk = xb + _zero_tile_from(done, xb.shape)
            keep_state(h)
            mix = mix_rows(jnp.concatenate(y_lru, axis=0), jnp.concatenate(pooled, axis=0))
            conv_and_gates(jnp.concatenate(proj, axis=1))
            residual_norm(mix)
            x_scr[...] = x

        @pl.when(s == n_chunks)
        def _():
            residual_norm(recurrence_and_mix(s - 1))
    else:
        stage_a()
        residual_norm(recurrence_and_mix(s))

    @pl.when(s == n_chunks - 1)
    def _():
        for k in range(CONV_W - 1):
            j = HIST - (CONV_W - 1) + k
            cnew_ref[:, k, :] = ul_ext[j * nb:(j + 1) * nb, :]

    @pl.when(s == pl.num_programs(0) - 1)
    def _():
        for k in range(POOL_MAX_WIN - 1):
            j = HIST - (POOL_MAX_WIN - 1) + k
            pnew_ref[:, k, :] = up_ext[j * nb:(j + 1) * nb, :]


def _mixer(x, h0, cst, pst, win, cw, cb, gwt, gab, gxb, lam, pw, ps, wout, ln_g, ln_b,
           *, layer, state_layer, nb, start_pos):
    n_rows = x.shape[0]
    t_total = n_rows // nb
    tt = min(MIXER_TIME_TILE, t_total)
    assert n_rows % nb == 0 and t_total % tt == 0 and nb % V7X_SUBLANES == 0
    m = tt * nb
    n_chunks = t_total // tt
    n_steps = n_chunks + 1 if n_chunks > 1 else 1
    hr = HIST * nb
    gw = GROUP_WIDTH
    kern = functools.partial(_mixer_kernel, nb=nb, tt=tt, n_chunks=n_chunks, start_pos=start_pos,
                             layer=layer)
    stacked = lambda p: _layer_resident(p.shape, layer)
    out_blk = (lambda i: (jnp.maximum(i - 1, 0), 0)) if n_chunks > 1 else (lambda i: (0, 0))
    return pl.pallas_call(
        kern,
        grid=(n_steps,),
        in_specs=[pl.BlockSpec((m, D_MODEL), lambda i: (jnp.minimum(i, n_chunks - 1), 0)),
                  _layer_resident(h0.shape, state_layer), _layer_resident(cst.shape, state_layer),
                  _layer_resident(pst.shape, state_layer),
                  stacked(win), stacked(cw), _resident(cb.shape), stacked(gwt),
                  _resident(gab.shape), _resident(gxb.shape), _resident(lam.shape),
                  stacked(pw), _resident(ps.shape), stacked(wout), stacked(ln_g), stacked(ln_b)],
        out_specs=[pl.BlockSpec((m, D_MODEL), out_blk),
                   pl.BlockSpec((nb, gw), lambda i: (0, 0)),
                   pl.BlockSpec((nb, CONV_W - 1, gw), lambda i: (0, 0, 0)),
                   pl.BlockSpec((nb, POOL_MAX_WIN - 1, gw), lambda i: (0, 0, 0))],
        out_shape=[jax.ShapeDtypeStruct(x.shape, _F32),
                   jax.ShapeDtypeStruct((nb, gw), _F32),
                   jax.ShapeDtypeStruct((nb, CONV_W - 1, gw), _F32),
                   jax.ShapeDtypeStruct((nb, POOL_MAX_WIN - 1, gw), _F32)],
        scratch_shapes=[pltpu.VMEM((hr + m, gw), _F32),
                        pltpu.VMEM((hr + m, gw), _F32),
                        pltpu.VMEM((nb, gw), _F32),
                        pltpu.VMEM((m, gw), _F32),
                        pltpu.VMEM((m, gw), _F32),
                        pltpu.VMEM((m, gw), _F32),
                        pltpu.VMEM((m, gw), _F32),
                        pltpu.VMEM((m, gw), _F32),
                        pltpu.VMEM((m, D_MODEL), _F32)],
        compiler_params=pltpu.CompilerParams(
            dimension_semantics=("arbitrary",),
            vmem_limit_bytes=V7X_VMEM_LIMIT_BYTES),
        name="mixer",
    )(x, h0, cst, pst, win, cw, cb, gwt, gab, gxb, lam, pw, ps, wout, ln_g, ln_b)


def _block_diag_halves(w):
    groups, n = w.shape[1], w.shape[2]
    gh = groups // 2
    halves = []
    for p in range(2):
        rows = [jnp.pad(w[:, p * gh + g], ((0, 0), (0, 0), (g * n, (gh - 1 - g) * n)))
                for g in range(gh)]
        halves.append(jnp.concatenate(rows, axis=1))
    return jnp.stack(halves, axis=1)


def kernel(x_prompt, x_sample, state_lru_h, state_conv, state_pool, ln_g, ln_b,
           w1_gate, w1_up, w1_down, w_in, conv_w, conv_b, gate_a_w, gate_a_b,
           gate_x_w, gate_x_b, lru_lambda, pool_w, pool_scale, w_out,
           w2_gate, w2_up, w2_down):
    batch, seq, d = x_prompt.shape
    dec_batch = x_sample.shape[0]
    gw = GROUP_WIDTH

    win = w_in.astype(_BF16)
    wout = w_out.astype(_BF16)
    gate_w = jnp.concatenate(
        [_block_diag_halves(gate_a_w), _block_diag_halves(gate_x_w)], axis=-1).astype(_BF16)
    pool_wb = _block_diag_halves(pool_w).astype(_BF16)
    mix_params = (win, conv_w, conv_b, gate_w, gate_a_b, gate_x_b, lru_lambda, pool_wb,
                  pool_scale, wout, ln_g, ln_b)

    xp = x_prompt
    xs = x_sample.reshape(dec_batch, d)
    zeros_h = jnp.zeros((1, batch, gw), _F32)
    zeros_conv = jnp.zeros((1, batch, CONV_W - 1, gw), _F32)
    zeros_pool = jnp.zeros((1, batch, POOL_MAX_WIN - 1, gw), _F32)

    p_h, p_conv, p_pool, s_h, s_conv, s_pool = [], [], [], [], [], []
    for l in range(DEPTH):
        xp, xs = _ffn(xp, xs, w1_gate, w1_up, w1_down, ln_g, ln_b, layer=l, ln_row=0,
                      layout="batch_to_time" if l == 0 else "rows")
        xp = xp.reshape(seq * batch, d)
        xp, h_t, cnew, pnew = _mixer(xp, zeros_h, zeros_conv, zeros_pool, *mix_params,
                                     layer=l, state_layer=0, nb=batch, start_pos=0)
        p_h.append(h_t); p_conv.append(cnew); p_pool.append(pnew)
        xs, h_t, cnew, pnew = _mixer(xs, state_lru_h, state_conv, state_pool, *mix_params,
                                     layer=l, state_layer=l, nb=dec_batch, start_pos=PAST_LEN)
        s_h.append(h_t); s_conv.append(cnew); s_pool.append(pnew)
        if l == DEPTH - 1:
            xp = xp.reshape(seq, batch, d)
        xp, xs = _ffn(xp, xs, w2_gate, w2_up, w2_down, ln_g, ln_b, layer=l, ln_row=2,
                      layout="time_to_batch" if l == DEPTH - 1 else "rows")

    y_sample = xs.reshape(dec_batch, 1, d)
    return (xp, y_sample,
            jnp.stack(p_h), jnp.stack(p_conv), jnp.stack(p_pool),
            jnp.stack(s_h), jnp.stack(s_conv), jnp.stack(s_pool))
```

```python
import functools
import math

import jax
import jax.numpy as jnp
from jax import lax
from jax.experimental import pallas as pl
from jax.experimental.pallas import tpu as pltpu

D_MODEL = 1024
DEPTH = 4
GROUP_WIDTH = 512
LRU_C = 8.0
CONV_W = 4
POOL_WINDOWS = (2, 4, 8, 16)
POOL_MAX_WIN = 16
D_FF = 2816
LN_EPS = 1e-5
DEEPNORM_ALPHA = (2.0 * DEPTH) ** 0.25
PAST_LEN = 16384
assert math.frexp(LRU_C)[0] == 0.5 and all(math.frexp(w)[0] == 0.5 for w in POOL_WINDOWS)

V7X_LANES = 128
V7X_SUBLANES = 8
V7X_MXU_DIM = 256
V7X_VMEM_LIMIT_BYTES = 56 * 1024 * 1024

HIST = POOL_MAX_WIN
FFN_TOKEN_TILE = 512
MIXER_TIME_TILE = 64
FFN_CHUNKS = (768, 768, 768, 512)
assert sum(FFN_CHUNKS) == D_FF and all(c % V7X_MXU_DIM == 0 for c in FFN_CHUNKS)
FFN_LOAD_SLOTS = 2

_BF16 = jnp.bfloat16
_F32 = jnp.float32


def _dot(a, b):
    return jnp.dot(a, b, preferred_element_type=_F32)


def _layernorm(y, g, b):
    mu = jnp.mean(y, axis=-1, keepdims=True)
    d = y - mu
    var = jnp.mean(d * d, axis=-1, keepdims=True)
    return d * lax.rsqrt(var + LN_EPS) * g + b


def _zero_tile_from(v, shape):
    u = lax.bitcast_convert_type(v, jnp.uint32)
    rows, cols = u.shape
    bf16_rows = 2 * V7X_SUBLANES
    acc = u[0:bf16_rows, :]
    for k in range(1, rows // bf16_rows):
        acc = acc | u[k * bf16_rows:(k + 1) * bf16_rows, :]
    word = acc[:, 0:V7X_LANES]
    for k in range(1, cols // V7X_LANES):
        word = word | acc[:, k * V7X_LANES:(k + 1) * V7X_LANES]
    zero = lax.bitcast_convert_type((word >> 16) >> 16, _F32).astype(_BF16)
    zero = jnp.concatenate([zero] * (shape[1] // V7X_LANES), axis=1)
    return jnp.concatenate([zero] * (shape[0] // bf16_rows), axis=0)


def _resident(shape):
    nd = len(shape)
    return pl.BlockSpec(shape, lambda *_: (0,) * nd, pipeline_mode=pl.Buffered(1))


def _layer_resident(shape, layer):
    nd = len(shape)
    return pl.BlockSpec((None,) + tuple(shape[1:]), lambda *_: (layer,) + (0,) * (nd - 1),
                        pipeline_mode=pl.Buffered(1))


def _ffn_weight_loader(layer, wg_hbm, wu_hbm, wd_hbm, wg_v, wu_v, wd_v, st_gu, st_d, sems):
    offsets = [sum(FFN_CHUNKS[:p]) for p in range(len(FFN_CHUNKS))]

    def copies(p):
        slot = p % FFN_LOAD_SLOTS
        cols, head = pl.ds(offsets[p], FFN_CHUNKS[p]), pl.ds(0, FFN_CHUNKS[p])
        return (pltpu.make_async_copy(wg_hbm.at[layer, :, cols], st_gu.at[slot, 0, :, head], sems.at[slot, 0]),
                pltpu.make_async_copy(wu_hbm.at[layer, :, cols], st_gu.at[slot, 1, :, head], sems.at[slot, 1]),
                pltpu.make_async_copy(wd_hbm.at[layer, cols, :], st_d.at[slot, head, :], sems.at[slot, 2]))

    def start(p):
        if p < len(FFN_CHUNKS):
            for cp in copies(p):
                cp.start(priority=1)

    def finish(p):
        slot = p % FFN_LOAD_SLOTS
        for cp in copies(p):
            cp.wait()
        lo, w = offsets[p], FFN_CHUNKS[p]
        wg_v[:, lo:lo + w] = st_gu[slot, 0, :, 0:w].astype(_BF16)
        wu_v[:, lo:lo + w] = st_gu[slot, 1, :, 0:w].astype(_BF16)
        wd_v[lo:lo + w, :] = st_d[slot, 0:w, :].astype(_BF16)

    return start, finish


def _swiglu(xb, wg_v, wu_v, wd_v, per_chunk=None, before_chunk=None):
    acc = jnp.zeros((xb.shape[0], D_MODEL), _F32)
    lo = 0
    for ci, width in enumerate(FFN_CHUNKS):
        if before_chunk is not None:
            before_chunk(ci)
        gate = _dot(xb, wg_v[:, lo:lo + width])
        up = _dot(xb, wu_v[:, lo:lo + width])
        hid = (jax.nn.silu(gate) * up).astype(_BF16)
        if per_chunk is not None:
            hid = per_chunk(ci, hid)
        acc = acc + _dot(hid, wd_v[lo:lo + width, :])
        lo += width
    return acc


def _ffn_kernel(xp_ref, xs_ref, wg_hbm, wu_hbm, wd_hbm, g_ref, b_ref, op_ref, os_ref,
                wg_v, wu_v, wd_v, st_gu, st_d, sems, y_scr, ys_scr,
                *, layer, ln_row, n_tiles, layout):
    s = pl.program_id(0)
    ln = lambda y: _layernorm(y, g_ref[ln_row:ln_row + 1, :], b_ref[ln_row:ln_row + 1, :])

    if layout == "rows":
        load_tile = lambda: xp_ref[...]
        def store_rows(r0, r1, val):
            op_ref[r0:r1, :] = val
    elif layout == "batch_to_time":
        nbat, tt = xp_ref.shape[0], xp_ref.shape[1]
        load_tile = lambda: xp_ref[...].reshape(nbat * tt, D_MODEL)
        def store_rows(r0, r1, val):
            for b in range(r0 // tt, r1 // tt):
                op_ref[:, b, :] = val[b * tt - r0:(b + 1) * tt - r0, :]
    else:
        tt, nbat = xp_ref.shape[0], xp_ref.shape[1]
        load_tile = lambda: jnp.concatenate([xp_ref[:, b, :] for b in range(nbat)], axis=0)
        def store_rows(r0, r1, val):
            op_ref[r0 // tt:r1 // tt, :, :] = val.reshape((r1 - r0) // tt, tt, D_MODEL)

    def prompt_tile(before_chunk=None):
        x = load_tile()
        part = x.shape[0] // len(FFN_CHUNKS)

        def ln_part(ci, hid):
            rows = slice(ci * part, (ci + 1) * part)
            normed = ln(y_scr[rows, :])
            store_rows(ci * part, (ci + 1) * part, normed)
            return hid + _zero_tile_from(normed, hid.shape)

        acc = _swiglu(x.astype(_BF16), wg_v, wu_v, wd_v, ln_part, before_chunk)
        y_scr[...] = DEEPNORM_ALPHA * x + 0.5 * acc

    @pl.when(s == 0)
    def _():
        y_scr[...] = jnp.zeros(y_scr.shape, _F32)
        start, finish = _ffn_weight_loader(layer, wg_hbm, wu_hbm, wd_hbm, wg_v, wu_v, wd_v,
                                           st_gu, st_d, sems)
        for p in range(FFN_LOAD_SLOTS):
            start(p)

        def load_chunk(ci):
            finish(ci)
            start(ci + FFN_LOAD_SLOTS)

        prompt_tile(load_chunk)

    @pl.when((s > 0) & (s < n_tiles))
    def _():
        prompt_tile()

    @pl.when(s == n_tiles)
    def _():
        store_rows(0, y_scr.shape[0], ln(y_scr[...]))
        xs = xs_ref[...]
        ys_scr[...] = DEEPNORM_ALPHA * xs + 0.5 * _swiglu(xs.astype(_BF16), wg_v, wu_v, wd_v)

    @pl.when(s == n_tiles + 1)
    def _():
        os_ref[...] = ln(ys_scr[...])


def _ffn(xp, xs, wg, wu, wd, ln_g, ln_b, *, layer, ln_row, layout="rows"):
    ms = xs.shape[0]
    tm = FFN_TOKEN_TILE
    assert tm % (len(FFN_CHUNKS) * 2 * V7X_SUBLANES) == 0
    if layout == "rows":
        m = xp.shape[0]
        in_block, out_block, out_full = (tm, D_MODEL), (tm, D_MODEL), (m, D_MODEL)
        at = lambda j: (j, 0)
        in_at = out_at = at
    else:
        nbat, t_total = (xp.shape[0], xp.shape[1]) if layout == "batch_to_time" else (xp.shape[1], xp.shape[0])
        m = nbat * t_total
        tt = tm // nbat
        part_rows = tm // len(FFN_CHUNKS)
        assert tm % nbat == 0 and tt % V7X_SUBLANES == 0 and part_rows % tt == 0
        batch_major = ((nbat, tt, D_MODEL), lambda j: (0, j, 0), (nbat, t_total, D_MODEL))
        time_major = ((tt, nbat, D_MODEL), lambda j: (j, 0, 0), (t_total, nbat, D_MODEL))
        src, dst = (batch_major, time_major) if layout == "batch_to_time" else (time_major, batch_major)
        in_block, in_at = src[0], src[1]
        out_block, out_at, out_full = dst
    assert m % tm == 0
    n_tiles = m // tm
    hbm = pl.BlockSpec(memory_space=pl.ANY)
    return pl.pallas_call(
        functools.partial(_ffn_kernel, layer=layer, ln_row=ln_row, n_tiles=n_tiles, layout=layout),
        grid=(n_tiles + 2,),
        in_specs=[
            pl.BlockSpec(in_block, lambda i: in_at(jnp.minimum(i, n_tiles - 1))),
            pl.BlockSpec((ms, D_MODEL), lambda i: (0, 0)),
            hbm, hbm, hbm,
            _layer_resident(ln_g.shape, layer), _layer_resident(ln_b.shape, layer),
        ],
        out_specs=[pl.BlockSpec(out_block, lambda i: out_at(jnp.clip(i - 1, 0, n_tiles - 1))),
                   pl.BlockSpec((ms, D_MODEL), lambda i: (0, 0))],
        out_shape=[jax.ShapeDtypeStruct(out_full, _F32),
                   jax.ShapeDtypeStruct((ms, D_MODEL), _F32)],
        scratch_shapes=[pltpu.VMEM((D_MODEL, D_FF), _BF16),
                        pltpu.VMEM((D_MODEL, D_FF), _BF16),
                        pltpu.VMEM((D_FF, D_MODEL), _BF16),
                        pltpu.VMEM((FFN_LOAD_SLOTS, 2, D_MODEL, max(FFN_CHUNKS)), _F32),
                        pltpu.VMEM((FFN_LOAD_SLOTS, max(FFN_CHUNKS), D_MODEL), _F32),
                        pltpu.SemaphoreType.DMA((FFN_LOAD_SLOTS, 3)),
                        pltpu.VMEM((tm, D_MODEL), _F32),
                        pltpu.VMEM((ms, D_MODEL), _F32)],
        compiler_params=pltpu.CompilerParams(
            dimension_semantics=("arbitrary",),
            vmem_limit_bytes=V7X_VMEM_LIMIT_BYTES),
        name="ffn",
    )(xp, xs, wg, wu, wd, ln_g, ln_b)


def _mixer_kernel(x_ref, h0_ref, cst_ref, pst_ref, win_ref, cw_ref, cb_ref, gw_ref, gab_ref,
                  gxb_ref, lam_ref, pw_ref, ps_ref, wout_ref, g_ref, b_ref,
                  o_ref, ht_ref, cnew_ref, pnew_ref,
                  ul_ext, up_ext, h_scr, ga_scr, gx_scr, xc_scr, hs_scr, gate_scr, x_scr,
                  *, nb, tt, n_chunks, start_pos, layer):
    s = pl.program_id(0)
    pipelined = n_chunks > 1
    m = tt * nb
    hr = HIST * nb
    gw = GROUP_WIDTH
    half = gw // 2
    vec = lambda ref: ref[layer:layer + 1, :]

    @pl.when(s == 0)
    def _():
        h_scr[...] = h0_ref[...]
        ul_ext[0:hr, :] = jnp.zeros((hr, gw), _F32)
        for k in range(CONV_W - 1):
            j = HIST - (CONV_W - 1) + k
            ul_ext[j * nb:(j + 1) * nb, :] = cst_ref[:, k, :]
        up_ext[0:nb, :] = jnp.zeros((nb, gw), _F32)
        for k in range(POOL_MAX_WIN - 1):
            j = HIST - (POOL_MAX_WIN - 1) + k
            up_ext[j * nb:(j + 1) * nb, :] = pst_ref[:, k, :]

    def project():
        x = x_ref[...]
        return x, _dot(x.astype(_BF16), win_ref[...])

    def conv_and_gates(proj):
        u_lru = proj[:, 0:gw]
        gate_scr[...] = proj[:, gw:2 * gw]
        ul_ext[hr:hr + m, :] = u_lru
        up_ext[hr:hr + m, :] = proj[:, 2 * gw:3 * gw]
        xc = vec(cb_ref) + cw_ref[CONV_W - 1:CONV_W, :] * u_lru
        for k in range(CONV_W - 1):
            off = hr - (CONV_W - 1 - k) * nb
            xc = xc + cw_ref[k:k + 1, :] * ul_ext[off:off + m, :]
        xc_scr[...] = xc
        xcb = xc.astype(_BF16)
        for p in range(2):
            cols = slice(p * half, (p + 1) * half)
            gts = _dot(xcb[:, cols], gw_ref[p])
            ga_scr[:, cols] = gts[:, 0:half]
            gx_scr[:, cols] = gts[:, half:2 * half]
        ul_ext[0:hr, :] = ul_ext[m:m + hr, :]

    def recurrence_and_mix(chunk):
        xc = xc_scr[...]
        r = jax.nn.sigmoid(ga_scr[...] + vec(gab_ref))
        i = jax.nn.sigmoid(gx_scr[...] + vec(gxb_ref))
        nl = -vec(lam_ref)
        softplus_nl = jnp.maximum(nl, 0.0) + jnp.log1p(jnp.exp(-jnp.abs(nl)))
        log_a = r * (-LRU_C * softplus_nl)
        a = jnp.exp(log_a)
        one_m_a2 = 1.0 - a * a
        mult = jnp.where(one_m_a2 > 0.0, one_m_a2 * lax.rsqrt(one_m_a2), one_m_a2)
        uu = mult * (i * xc)
        h = h_scr[...]
        for t in range(tt):
            rows = slice(t * nb, (t + 1) * nb)
            h = a[rows, :] * h + uu[rows, :]
            hs_scr[rows, :] = h
        h_scr[...] = h
        ht_ref[...] = h
        y_lru = hs_scr[...] * jax.nn.gelu(gate_scr[...])

        t_loc = lax.broadcasted_iota(jnp.int32, (tt, nb, V7X_LANES), 0).reshape(m, V7X_LANES)
        pos = start_pos + chunk * tt + t_loc
        pooled = []
        for gi, w in enumerate(POOL_WINDOWS):
            lanes = slice(gi * V7X_LANES, (gi + 1) * V7X_LANES)
            n_rows = (w - 1) * nb + m
            sums = up_ext[hr - (w - 1) * nb:hr + m, lanes]
            span = 1
            while span < w:
                n_rows -= span * nb
                sums = sums[span * nb:span * nb + n_rows, :] + sums[0:n_rows, :]
                span *= 2
            head = min(m, (w - 1) * nb)
            cnt = jnp.minimum(w, pos[0:head, :] + 1).astype(_F32)
            mean = sums[0:head, :] / cnt
            if head < m:
                mean = jnp.concatenate([mean, sums[head:m, :] * (1.0 / w)], axis=0)
            pooled.append(mean - up_ext[hr:hr + m, lanes])
        pooled = jnp.concatenate(pooled, axis=-1).astype(_BF16)
        y_pool = jnp.concatenate(
            [_dot(pooled[:, p * half:(p + 1) * half], pw_ref[p]) for p in range(2)],
            axis=-1) * vec(ps_ref)

        mix = _dot(jnp.concatenate([y_lru, y_pool], axis=-1).astype(_BF16), wout_ref[...])
        up_ext[0:hr, :] = up_ext[m:m + hr, :]
        return mix

    def residual_norm(mix):
        y = DEEPNORM_ALPHA * x_scr[...] + mix
        o_ref[...] = _layernorm(y, g_ref[1:2, :], b_ref[1:2, :])

    def stage_a():
        x, proj = project()
        x_scr[...] = x
        conv_and_gates(proj)

    if pipelined:
        @pl.when(s == 0)
        def _():
            stage_a()

        @pl.when((s > 0) & (s < n_chunks))
        def _():
            x, proj = project()
            mix = recurrence_and_mix(s - 1)
            conv_and_gates(proj)
            residual_norm(mix)
            x_scr[...] = x

        @pl.when(s == n_chunks)
        def _():
            residual_norm(recurrence_and_mix(s - 1))
    else:
        stage_a()
        residual_norm(recurrence_and_mix(s))

    @pl.when(s == n_chunks - 1)
    def _():
        for k in range(CONV_W - 1):
            j = HIST - (CONV_W - 1) + k
            cnew_ref[:, k, :] = ul_ext[j * nb:(j + 1) * nb, :]

    @pl.when(s == pl.num_programs(0) - 1)
    def _():
        for k in range(POOL_MAX_WIN - 1):
            j = HIST - (POOL_MAX_WIN - 1) + k
            pnew_ref[:, k, :] = up_ext[j * nb:(j + 1) * nb, :]


def _mixer(x, h0, cst, pst, win, cw, cb, gwt, gab, gxb, lam, pw, ps, wout, ln_g, ln_b,
           *, layer, state_layer, nb, start_pos):
    n_rows = x.shape[0]
    t_total = n_rows // nb
    tt = min(MIXER_TIME_TILE, t_total)
    assert n_rows % nb == 0 and t_total % tt == 0 and nb % V7X_SUBLANES == 0
    m = tt * nb
    n_chunks = t_total // tt
    n_steps = n_chunks + 1 if n_chunks > 1 else 1
    hr = HIST * nb
    gw = GROUP_WIDTH
    kern = functools.partial(_mixer_kernel, nb=nb, tt=tt, n_chunks=n_chunks, start_pos=start_pos,
                             layer=layer)
    stacked = lambda p: _layer_resident(p.shape, layer)
    out_blk = (lambda i: (jnp.maximum(i - 1, 0), 0)) if n_chunks > 1 else (lambda i: (0, 0))
    return pl.pallas_call(
        kern,
        grid=(n_steps,),
        in_specs=[pl.BlockSpec((m, D_MODEL), lambda i: (jnp.minimum(i, n_chunks - 1), 0)),
                  _layer_resident(h0.shape, state_layer), _layer_resident(cst.shape, state_layer),
                  _layer_resident(pst.shape, state_layer),
                  stacked(win), stacked(cw), _resident(cb.shape), stacked(gwt),
                  _resident(gab.shape), _resident(gxb.shape), _resident(lam.shape),
                  stacked(pw), _resident(ps.shape), stacked(wout), stacked(ln_g), stacked(ln_b)],
        out_specs=[pl.BlockSpec((m, D_MODEL), out_blk),
                   pl.BlockSpec((nb, gw), lambda i: (0, 0)),
                   pl.BlockSpec((nb, CONV_W - 1, gw), lambda i: (0, 0, 0)),
                   pl.BlockSpec((nb, POOL_MAX_WIN - 1, gw), lambda i: (0, 0, 0))],
        out_shape=[jax.ShapeDtypeStruct(x.shape, _F32),
                   jax.ShapeDtypeStruct((nb, gw), _F32),
                   jax.ShapeDtypeStruct((nb, CONV_W - 1, gw), _F32),
                   jax.ShapeDtypeStruct((nb, POOL_MAX_WIN - 1, gw), _F32)],
        scratch_shapes=[pltpu.VMEM((hr + m, gw), _F32),
                        pltpu.VMEM((hr + m, gw), _F32),
                        pltpu.VMEM((nb, gw), _F32),
                        pltpu.VMEM((m, gw), _F32),
                        pltpu.VMEM((m, gw), _F32),
                        pltpu.VMEM((m, gw), _F32),
                        pltpu.VMEM((m, gw), _F32),
                        pltpu.VMEM((m, gw), _F32),
                        pltpu.VMEM((m, D_MODEL), _F32)],
        compiler_params=pltpu.CompilerParams(
            dimension_semantics=("arbitrary",),
            vmem_limit_bytes=V7X_VMEM_LIMIT_BYTES),
        name="mixer",
    )(x, h0, cst, pst, win, cw, cb, gwt, gab, gxb, lam, pw, ps, wout, ln_g, ln_b)


def _block_diag_halves(w):
    groups, n = w.shape[1], w.shape[2]
    gh = groups // 2
    halves = []
    for p in range(2):
        rows = [jnp.pad(w[:, p * gh + g], ((0, 0), (0, 0), (g * n, (gh - 1 - g) * n)))
                for g in range(gh)]
        halves.append(jnp.concatenate(rows, axis=1))
    return jnp.stack(halves, axis=1)


def kernel(x_prompt, x_sample, state_lru_h, state_conv, state_pool, ln_g, ln_b,
           w1_gate, w1_up, w1_down, w_in, conv_w, conv_b, gate_a_w, gate_a_b,
           gate_x_w, gate_x_b, lru_lambda, pool_w, pool_scale, w_out,
           w2_gate, w2_up, w2_down):
    batch, seq, d = x_prompt.shape
    dec_batch = x_sample.shape[0]
    gw = GROUP_WIDTH

    win = w_in.astype(_BF16)
    wout = w_out.astype(_BF16)
    gate_w = jnp.concatenate(
        [_block_diag_halves(gate_a_w), _block_diag_halves(gate_x_w)], axis=-1).astype(_BF16)
    pool_wb = _block_diag_halves(pool_w).astype(_BF16)
    mix_params = (win, conv_w, conv_b, gate_w, gate_a_b, gate_x_b, lru_lambda, pool_wb,
                  pool_scale, wout, ln_g, ln_b)

    xp = x_prompt
    xs = x_sample.reshape(dec_batch, d)
    zeros_h = jnp.zeros((1, batch, gw), _F32)
    zeros_conv = jnp.zeros((1, batch, CONV_W - 1, gw), _F32)
    zeros_pool = jnp.zeros((1, batch, POOL_MAX_WIN - 1, gw), _F32)

    p_h, p_conv, p_pool, s_h, s_conv, s_pool = [], [], [], [], [], []
    for l in range(DEPTH):
        xp, xs = _ffn(xp, xs, w1_gate, w1_up, w1_down, ln_g, ln_b, layer=l, ln_row=0,
                      layout="batch_to_time" if l == 0 else "rows")
        xp = xp.reshape(seq * batch, d)
        xp, h_t, cnew, pnew = _mixer(xp, zeros_h, zeros_conv, zeros_pool, *mix_params,
                                     layer=l, state_layer=0, nb=batch, start_pos=0)
        p_h.append(h_t); p_conv.append(cnew); p_pool.append(pnew)
        xs, h_t, cnew, pnew = _mixer(xs, state_lru_h, state_conv, state_pool, *mix_params,
                                     layer=l, state_layer=l, nb=dec_batch, start_pos=PAST_LEN)
        s_h.append(h_t); s_conv.append(cnew); s_pool.append(pnew)
        if l == DEPTH - 1:
            xp = xp.reshape(seq, batch, d)
        xp, xs = _ffn(xp, xs, w2_gate, w2_up, w2_down, ln_g, ln_b, layer=l, ln_row=2,
                      layout="time_to_batch" if l == DEPTH - 1 else "rows")

    y_sample = xs.reshape(dec_batch, 1, d)
    return (xp, y_sample,
            jnp.stack(p_h), jnp.stack(p_conv), jnp.stack(p_pool),
            jnp.stack(s_h), jnp.stack(s_conv), jnp.stack(s_pool))
```
